```python
import jax, jax.numpy as jnp
from jax import lax
import numpy as np

D_MODEL = 2048
BATCH = 4
SEQ = 4096
DEPTH = 4

N_MIXERS = 3
BLOCK = 128

SB_HEADS = 16
SB_HEAD_DIM = D_MODEL // SB_HEADS

ML_HEADS = 8
ML_V_DIM = D_MODEL // ML_HEADS
ML_QK_DIM = ML_V_DIM // 2
ML_PROJ = 2 * ML_HEADS * ML_QK_DIM + 2 * ML_HEADS * ML_V_DIM + 2 * ML_HEADS

SG_WIDTH = D_MODEL
SG_GROUPS = 8
SG_GROUP_DIM = SG_WIDTH // SG_GROUPS

PEER_HEADS = 8
PEER_N_KEYS = 128
PEER_N_EXPERTS = PEER_N_KEYS * PEER_N_KEYS
PEER_KEY_DIM = 256
PEER_TOPK = 16
PEER_TOKEN_BLOCK = 128

DEEPNORM_ALPHA = (2 * DEPTH) ** 0.25
DEEPNORM_BETA = (8 * DEPTH) ** -0.25
LN_EPS = 1e-5

kernel_name = 'hybrid_sb_mlstm_sgu_peer_trunk'


def layer_norm(x, g, b):
    xf = x.astype(jnp.float32)
    mu = jnp.mean(xf, axis=-1, keepdims=True)
    var = jnp.mean(jnp.square(xf - mu), axis=-1, keepdims=True)
    return ((xf - mu) * lax.rsqrt(var + LN_EPS) * g + b).astype(x.dtype)


def stick_breaking_attention(x, w_in, w_out):
    B, S, _ = x.shape
    H, dh, L = SB_HEADS, SB_HEAD_DIM, BLOCK
    nb = S // L
    q, k, v = jnp.split((x @ w_in).astype(jnp.float32), 3, axis=-1)
    q = q.reshape(B, nb, L, H, dh).transpose(1, 0, 3, 2, 4) * dh ** -0.5
    k = k.reshape(B, S, H, dh).transpose(0, 2, 1, 3)
    v = v.reshape(B, S, H, dh).transpose(0, 2, 1, 3)
    key_pos = jnp.arange(S)

    def query_block(args):
        qb, blk = args
        q_pos = blk * L + jnp.arange(L)
        z = jnp.einsum('bhqd,bhkd->bhqk', qb, k)
        strict = key_pos[None, :] < q_pos[:, None]
        log_keep = jnp.where(strict, -jax.nn.softplus(z), 0.0)
        log_between = lax.cumsum(log_keep, axis=3, reverse=True) - log_keep
        w = jnp.where(strict, jnp.exp(jax.nn.log_sigmoid(z) + log_between), 0.0)
        return jnp.einsum('bhqk,bhkd->bhqd', w, v)

    o = lax.map(query_block, (q, jnp.arange(nb)))
    o = o.transpose(1, 0, 3, 2, 4).reshape(B, S, H * dh)
    return o.astype(x.dtype) @ w_out


def mlstm_memory(x, w_in, b_gates, norm_g, w_out):
    B, S, _ = x.shape
    H, dk, dv, L = ML_HEADS, ML_QK_DIM, ML_V_DIM, BLOCK
    nc = S // L
    splits = (H * dk, 2 * H * dk, 2 * H * dk + H * dv, 2 * H * dk + 2 * H * dv)
    q, k, v, o_pre, g_pre = jnp.split((x @ w_in).astype(jnp.float32), splits, axis=-1)
    g_pre = g_pre + b_gates.astype(jnp.float32)
    log_i = g_pre[..., :H]
    log_f = jax.nn.log_sigmoid(g_pre[..., H:])

    def to_chunks(t, d):
        return t.reshape(B, nc, L, H, d).transpose(1, 0, 3, 2, 4)

    def gate_chunks(t):
        return t.reshape(B, nc, L, H).transpose(1, 0, 3, 2)

    xs = (to_chunks(q, dk) * dk ** -0.5, to_chunks(k, dk), to_chunks(v, dv),
          gate_chunks(log_i), gate_chunks(log_f))
    causal = jnp.tril(jnp.ones((L, L), dtype=bool))

    def chunk_step(carry, inp):
        C, n, m = carry
        qc, kc, vc, li, lf = inp
        b = jnp.cumsum(lf, axis=-1)
        d_intra = jnp.where(causal, b[..., :, None] - b[..., None, :] + li[..., None, :], -jnp.inf)
        d_inter = b + m[..., None]
        m_t = jnp.maximum(d_inter, jnp.max(d_intra, axis=-1))
        w_intra = jnp.exp(d_intra - m_t[..., None])
        w_inter = jnp.exp(d_inter - m_t)
        s = jnp.einsum('bhqd,bhkd->bhqk', qc, kc) * w_intra
        num = jnp.einsum('bhqk,bhkd->bhqd', s, vc) + w_inter[..., None] * jnp.einsum('bhqd,bhde->bhqe', qc, C)
        den = jnp.sum(s, axis=-1) + w_inter * jnp.einsum('bhqd,bhd->bhq', qc, n)
        h = num / jnp.maximum(jnp.abs(den), jnp.exp(-m_t))[..., None]
        b_last = b[..., -1]
        d_state = b_last[..., None] - b + li
        m_new = jnp.maximum(b_last + m, jnp.max(d_state, axis=-1))
        decay = jnp.exp(b_last + m - m_new)
        w_state = jnp.exp(d_state - m_new[..., None])
        C_new = decay[..., None, None] * C + jnp.einsum('bhl,bhld,bhle->bhde', w_state, kc, vc)
        n_new = decay[..., None] * n + jnp.einsum('bhl,bhld->bhd', w_state, kc)
        return (C_new, n_new, m_new), h

    init = (jnp.zeros((B, H, dk, dv), jnp.float32), jnp.zeros((B, H, dk), jnp.float32),
            jnp.zeros((B, H), jnp.float32))
    _, h = lax.scan(chunk_step, init, xs)
    h = h.transpose(1, 0, 3, 2, 4)
    h = h * lax.rsqrt(jnp.mean(h * h, axis=-1, keepdims=True) + LN_EPS)
    h = h.reshape(B, S, H * dv) * norm_g
    y = jax.nn.sigmoid(o_pre) * h
    return y.astype(x.dtype) @ w_out


def spatial_gating_mlp(x, w_in, norm_g, norm_b, w_s, b_s, w_out):
    B, S, _ = x.shape
    L, G, dg = BLOCK, SG_GROUPS, SG_GROUP_DIM
    nc = S // L
    u, v = jnp.split(jax.nn.gelu(x @ w_in, approximate=False), 2, axis=-1)
    v = layer_norm(v, norm_g, norm_b)
    vc = v.reshape(B, nc, L, G, dg)
    ws = jnp.where(jnp.tril(jnp.ones((L, L), dtype=bool)), w_s, 0.0)
    mixed = jnp.einsum('gts,bcsgd->bctgd', ws.astype(vc.dtype), vc) + b_s.T[None, None, :, :, None]
    y = u * mixed.reshape(B, S, SG_WIDTH).astype(u.dtype)
    return y @ w_out


def peer(x, w_q, sub_keys, u_tab, v_tab):
    B, S, D = x.shape
    T, H, K, NK = PEER_TOKEN_BLOCK, PEER_HEADS, PEER_TOPK, PEER_N_KEYS
    nb = (B * S) // T

    def token_block(xb):
        q = (xb @ w_q).reshape(T, H, 2, PEER_KEY_DIM // 2)
        scores = jnp.einsum('thpd,hpnd->thpn', q, sub_keys).astype(jnp.float32)
        s_top, i_top = lax.top_k(scores, K)
        cand = s_top[:, :, 0, :, None] + s_top[:, :, 1, None, :]
        cand_idx = i_top[:, :, 0, :, None] * NK + i_top[:, :, 1, None, :]
        best, pos = lax.top_k(cand.reshape(T, H, K * K), K)
        expert = jnp.take_along_axis(cand_idx.reshape(T, H, K * K), pos, axis=-1)
        gate = jax.nn.softmax(best, axis=-1)
        u = u_tab[expert]
        act = jax.nn.gelu(jnp.einsum('thkd,td->thk', u, xb).astype(jnp.float32), approximate=False)
        vsel = v_tab[expert]
        return jnp.einsum('thk,thkd->td', (gate * act).astype(vsel.dtype), vsel)

    out = lax.map(token_block, x.reshape(nb, T, D))
    return out.reshape(B, S, D)


def setup_inputs(seed: int = 0) -> dict:
    key = jax.random.key(seed)
    keys = iter(jax.random.split(key, 64))

    def normal(shape, scale):
        return jax.random.normal(next(keys), shape, jnp.float32) * scale

    def gain(n):
        return 1.0 + normal((n,), 0.02)

    d = D_MODEL
    inputs = {'x': normal((BATCH, SEQ, d), 1.0)}
    for i in range(DEPTH):
        p = 'l%d_' % i
        kind = i % N_MIXERS
        if kind == 0:
            inputs[p + 'sb_w_in'] = normal((d, 3 * d), d ** -0.5)
            inputs[p + 'sb_w_out'] = normal((d, d), DEEPNORM_BETA * d ** -0.5)
        elif kind == 1:
            inputs[p + 'ml_w_in'] = normal((d, ML_PROJ), d ** -0.5)
            forget_bias = jnp.linspace(3.0, 6.0, ML_HEADS, dtype=jnp.float32)
            inputs[p + 'ml_b_gates'] = jnp.concatenate(
                [normal((ML_HEADS,), 0.1), forget_bias + normal((ML_HEADS,), 0.1)])
            inputs[p + 'ml_norm_g'] = gain(ML_HEADS * ML_V_DIM)
            inputs[p + 'ml_w_out'] = normal((ML_HEADS * ML_V_DIM, d),
                                            DEEPNORM_BETA * (ML_HEADS * ML_V_DIM) ** -0.5)
        else:
            inputs[p + 'sg_w_in'] = normal((d, 2 * SG_WIDTH), d ** -0.5)
            inputs[p + 'sg_norm_g'] = gain(SG_WIDTH)
            inputs[p + 'sg_norm_b'] = normal((SG_WIDTH,), 0.02)
            inputs[p + 'sg_w_s'] = normal((SG_GROUPS, BLOCK, BLOCK), BLOCK ** -0.5)
            inputs[p + 'sg_b_s'] = 1.0 + normal((SG_GROUPS, BLOCK), 0.02)
            inputs[p + 'sg_w_out'] = normal((SG_WIDTH, d), DEEPNORM_BETA * SG_WIDTH ** -0.5)
        inputs[p + 'ln1_g'] = gain(d)
        inputs[p + 'ln1_b'] = normal((d,), 0.02)
        inputs[p + 'peer_w_q'] = normal((d, PEER_HEADS * PEER_KEY_DIM), d ** -0.5)
        inputs[p + 'peer_sub_keys'] = normal((PEER_HEADS, 2, PEER_N_KEYS, PEER_KEY_DIM // 2),
                                             (PEER_KEY_DIM // 2) ** -0.5)
        inputs[p + 'peer_u'] = normal((PEER_N_EXPERTS, d), d ** -0.5)
        inputs[p + 'peer_v'] = normal((PEER_N_EXPERTS, d), DEEPNORM_BETA * PEER_HEADS ** -0.5)
        inputs[p + 'ln2_g'] = gain(d)
        inputs[p + 'ln2_b'] = normal((d,), 0.02)
    return inputs


def reference(x,
              l0_sb_w_in, l0_sb_w_out, l0_ln1_g, l0_ln1_b,
              l0_peer_w_q, l0_peer_sub_keys, l0_peer_u, l0_peer_v, l0_ln2_g, l0_ln2_b,
              l1_ml_w_in, l1_ml_b_gates, l1_ml_norm_g, l1_ml_w_out, l1_ln1_g, l1_ln1_b,
              l1_peer_w_q, l1_peer_sub_keys, l1_peer_u, l1_peer_v, l1_ln2_g, l1_ln2_b,
              l2_sg_w_in, l2_sg_norm_g, l2_sg_norm_b, l2_sg_w_s, l2_sg_b_s, l2_sg_w_out, l2_ln1_g, l2_ln1_b,
              l2_peer_w_q, l2_peer_sub_keys, l2_peer_u, l2_peer_v, l2_ln2_g, l2_ln2_b,
              l3_sb_w_in, l3_sb_w_out, l3_ln1_g, l3_ln1_b,
              l3_peer_w_q, l3_peer_sub_keys, l3_peer_u, l3_peer_v, l3_ln2_g, l3_ln2_b):
    mixer_fns = (stick_breaking_attention, mlstm_memory, spatial_gating_mlp)
    mixer_params = (
        (l0_sb_w_in, l0_sb_w_out),
        (l1_ml_w_in, l1_ml_b_gates, l1_ml_norm_g, l1_ml_w_out),
        (l2_sg_w_in, l2_sg_norm_g, l2_sg_norm_b, l2_sg_w_s, l2_sg_b_s, l2_sg_w_out),
        (l3_sb_w_in, l3_sb_w_out),
    )
    norm1 = ((l0_ln1_g, l0_ln1_b), (l1_ln1_g, l1_ln1_b), (l2_ln1_g, l2_ln1_b), (l3_ln1_g, l3_ln1_b))
    peer_params = (
        (l0_peer_w_q, l0_peer_sub_keys, l0_peer_u, l0_peer_v),
        (l1_peer_w_q, l1_peer_sub_keys, l1_peer_u, l1_peer_v),
        (l2_peer_w_q, l2_peer_sub_keys, l2_peer_u, l2_peer_v),
        (l3_peer_w_q, l3_peer_sub_keys, l3_peer_u, l3_peer_v),
    )
    norm2 = ((l0_ln2_g, l0_ln2_b), (l1_ln2_g, l1_ln2_b), (l2_ln2_g, l2_ln2_b), (l3_ln2_g, l3_ln2_b))
    h = x
    for i in range(DEPTH):
        mixed = mixer_fns[i % N_MIXERS](h, *mixer_params[i])
        h = layer_norm(DEEPNORM_ALPHA * h + mixed, *norm1[i])
        h = layer_norm(DEEPNORM_ALPHA * h + peer(h, *peer_params[i]), *norm2[i])
    return h
```

```python
import functools

import jax
import jax.numpy as jnp
from jax import lax
from jax.experimental import pallas as pl
from jax.experimental.pallas import tpu as pltpu

F32 = jnp.float32
BF16 = jnp.bfloat16

LANES = 128
CHUNK = 128
SB_HEADS = 16
ML_HEADS = 8
SG_GROUPS = 8
PEER_HEADS = 8
PEER_KEYS = 128
PEER_TOPK = 16
DEPTH = 4
ALPHA = (2 * DEPTH) ** 0.25
LN_EPS = 1e-5
VMEM_LIMIT_BYTES = 56 * 1024 * 1024
EXP_ZERO_BELOW = -110.0
NOT_RANKED = 99.0


def _params(*sem):
    return pltpu.CompilerParams(dimension_semantics=sem, vmem_limit_bytes=VMEM_LIMIT_BYTES)


def _gelu(x):
    return 0.5 * x * (1.0 + lax.erf(x * 0.7071067811865476))


def _softplus(z):
    return jnp.maximum(z, 0.0) + jnp.log1p(jnp.exp(-jnp.abs(z)))


def _log_sigmoid(z):
    return -_softplus(-z)


def _layer_norm(t, g, b):
    mu = jnp.mean(t, axis=-1, keepdims=True)
    d = t - mu
    var = jnp.mean(d * d, axis=-1, keepdims=True)
    return d * lax.rsqrt(var + LN_EPS) * g + b


def _dot(a, b):
    return jnp.dot(a, b, preferred_element_type=F32)


def _dot_nt(a, b):
    return lax.dot_general(a, b, (((1,), (1,)), ((), ())), preferred_element_type=F32)


def _dot_tn(a, b):
    return lax.dot_general(a, b, (((0,), (0,)), ((), ())), preferred_element_type=F32)


def _mm_kernel(x_ref, w_ref, o_ref, *, act):
    y = _dot(x_ref[...], w_ref[...])
    if act == "gelu":
        y = _gelu(y)
    o_ref[...] = y.astype(o_ref.dtype)


def _matmul(x, w, out_dtype, act=None, tm=512, tn=512, name="matmul"):
    m, k = x.shape
    n = w.shape[1]
    tm, tn = min(tm, m), min(tn, n)
    assert m % tm == 0 and n % tn == 0, (m, n, tm, tn)
    return pl.pallas_call(
        functools.partial(_mm_kernel, act=act),
        grid=(n // tn, m // tm),
        in_specs=[pl.BlockSpec((tm, k), lambda j, i: (i, 0)),
                  pl.BlockSpec((k, tn), lambda j, i: (0, j))],
        out_specs=pl.BlockSpec((tm, tn), lambda j, i: (i, j)),
        out_shape=jax.ShapeDtypeStruct((m, n), out_dtype),
        compiler_params=_params("arbitrary", "arbitrary"),
        name=name,
    )(x, w)


def _proj_ln_kernel(x_ref, w_ref, h_ref, g_ref, b_ref, o_ref, ob_ref):
    y = _dot(x_ref[...], w_ref[...])
    out = _layer_norm(ALPHA * h_ref[...] + y, g_ref[...], b_ref[...])
    o_ref[...] = out
    ob_ref[...] = out.astype(BF16)


def _proj_residual_ln(x, w, h, g, b, tm=256):
    m, k = x.shape
    d = w.shape[1]
    tm = min(tm, m)
    assert m % tm == 0
    row = lambda i: (i, 0)
    fixed = lambda i: (0, 0)
    return pl.pallas_call(
        _proj_ln_kernel,
        grid=(m // tm,),
        in_specs=[pl.BlockSpec((tm, k), row), pl.BlockSpec((k, d), fixed),
                  pl.BlockSpec((tm, d), row), pl.BlockSpec((1, d), fixed),
                  pl.BlockSpec((1, d), fixed)],
        out_specs=[pl.BlockSpec((tm, d), row), pl.BlockSpec((tm, d), row)],
        out_shape=[jax.ShapeDtypeStruct((m, d), F32), jax.ShapeDtypeStruct((m, d), BF16)],
        compiler_params=_params("arbitrary"),
        name="proj_residual_ln",
    )(x, w, h, g.reshape(1, d), b.reshape(1, d))


def _sb_kernel(q_ref, k_ref, v_ref, o_ref, *, blk, scale):
    qi = pl.program_id(2)
    q = q_ref[...]
    dh = q.shape[1]
    row = lax.broadcasted_iota(jnp.int32, (blk, blk), 0)
    col = lax.broadcasted_iota(jnp.int32, (blk, blk), 1)
    suffix = jnp.concatenate(
        [(row > col).astype(BF16), jnp.ones((blk, LANES), BF16)], axis=1)
    strict = col < row

    def key_block(kb, carry_sum, acc, diagonal):
        start = pl.multiple_of(kb * blk, blk)
        k = k_ref[pl.ds(start, blk), :]
        v = v_ref[pl.ds(start, blk), :]
        z = _dot_nt(q, k) * scale
        sp = _softplus(z)
        log_keep = jnp.where(strict, -sp, 0.0) if diagonal else -sp
        hi = log_keep.astype(BF16)
        lo = (log_keep - hi.astype(F32)).astype(BF16)
        sums = _dot(hi, suffix) + _dot(lo, suffix)
        between = sums[:, :blk] + jnp.concatenate([carry_sum] * (blk // LANES), axis=1)
        w = jnp.exp(z - sp + between)
        if diagonal:
            w = jnp.where(strict, w, 0.0)
        acc = acc + _dot(w.astype(BF16), v)
        return carry_sum + sums[:, blk:], acc

    carry_sum, acc = key_block(qi, jnp.zeros((blk, LANES), F32), jnp.zeros((blk, dh), F32), True)

    def cond(c):
        return jnp.logical_and(c[0] >= 0, c[1] > EXP_ZERO_BELOW)

    def body(c):
        kb, _, carry_sum, acc = c
        carry_sum, acc = key_block(kb, carry_sum, acc, False)
        return kb - 1, jnp.max(carry_sum), carry_sum, acc

    _, _, _, acc = lax.while_loop(cond, body, (qi - 1, jnp.max(carry_sum), carry_sum, acc))
    o_ref[...] = acc.astype(o_ref.dtype)


def _sb_attention(qkv, batch, seq, blk=CHUNK):
    n, three_d = qkv.shape
    d = three_d // 3
    heads = SB_HEADS
    dh = d // heads
    nq = seq // blk
    return pl.pallas_call(
        functools.partial(_sb_kernel, blk=blk, scale=dh ** -0.5),
        grid=(batch, heads, nq),
        in_specs=[pl.BlockSpec((blk, dh), lambda b, h, i: (b * nq + i, h)),
                  pl.BlockSpec((seq, dh), lambda b, h, i: (b, heads + h)),
                  pl.BlockSpec((seq, dh), lambda b, h, i: (b, 2 * heads + h))],
        out_specs=pl.BlockSpec((blk, dh), lambda b, h, i: (b * nq + i, h)),
        out_shape=jax.ShapeDtypeStruct((n, d), BF16),
        compiler_params=_params("arbitrary", "arbitrary", "arbitrary"),
        name="sb_attention",
    )(qkv, qkv, qkv)


def _mlstm_kernel(q_ref, k_ref, v_ref, o_ref, g_ref, bias_ref, ng_ref, y_ref, c_ref, m_ref,
                  *, dk, dv, scale):
    L = q_ref.shape[0]
    heads = ML_HEADS

    @pl.when(pl.program_id(1) == 0)
    def _():
        c_ref[...] = jnp.zeros_like(c_ref)
        m_ref[...] = jnp.zeros_like(m_ref)

    g = g_ref[...] + bias_ref[...]
    log_f = _log_sigmoid(g)
    row = lax.broadcasted_iota(jnp.int32, (L, L), 0)
    col = lax.broadcasted_iota(jnp.int32, (L, L), 1)
    causal = col <= row
    b_cols = jnp.dot(causal.astype(F32), log_f, precision=lax.Precision.HIGHEST,
                     preferred_element_type=F32)
    b_rows = b_cols.T
    g_rows = g.T
    ones_col = (lax.broadcasted_iota(jnp.int32, (L, LANES), 1) == 0).astype(BF16)

    for h in range(heads):
        q = q_ref[:, h * dk:(h + 1) * dk]
        k = k_ref[:, h * dk:(h + 1) * dk]
        v_ext = jnp.concatenate([v_ref[:, h * dv:(h + 1) * dv], ones_col], axis=1)
        li_col = g[:, h:h + 1]
        li_row = g_rows[h:h + 1, :]
        b_col = b_cols[:, heads + h:heads + h + 1]
        b_row = b_rows[heads + h:heads + h + 1, :]
        m_prev = m_ref[h:h + 1, 0:1]
        c_ext = c_ref[h]

        d_intra = jnp.where(causal, b_col - b_row + li_row, -jnp.inf)
        d_inter = b_col + m_prev
        m_t = jnp.maximum(d_inter, jnp.max(d_intra, axis=-1, keepdims=True))
        w_intra = jnp.exp(d_intra - m_t)
        w_inter = jnp.exp(d_inter - m_t)
        s = _dot_nt(q, k) * scale * w_intra
        num = _dot(s.astype(BF16), v_ext) + w_inter * (_dot(q, c_ext.astype(BF16)) * scale)
        den = num[:, dv:dv + 1]
        hid = num[:, :dv] / jnp.maximum(jnp.abs(den), jnp.exp(-m_t))

        b_last = b_col[L - 1:L, :]
        m_new = jnp.maximum(b_last + m_prev,
                            jnp.max(b_last - b_row + li_row, axis=-1, keepdims=True))
        decay = jnp.exp(b_last + m_prev - m_new)
        w_state = jnp.exp(b_last - b_col + li_col - m_new)
        kw = (k.astype(F32) * w_state).astype(BF16)
        c_ref[h] = decay * c_ext + _dot_tn(kw, v_ext)
        m_ref[h:h + 1, :] = jnp.broadcast_to(m_new, (1, LANES))

        hid = hid * lax.rsqrt(jnp.mean(hid * hid, axis=-1, keepdims=True) + LN_EPS)
        hid = hid * ng_ref[:, h * dv:(h + 1) * dv]
        gate = jax.nn.sigmoid(o_ref[:, h * dv:(h + 1) * dv])
        y_ref[:, h * dv:(h + 1) * dv] = (gate * hid).astype(y_ref.dtype)


def _mlstm(qkv, og, bias, norm_g, batch, seq):
    n = qkv.shape[0]
    heads = ML_HEADS
    d = og.shape[1] - LANES
    dv = d // heads
    dk = dv // 2
    L = CHUNK
    nc = seq // L
    tok = lambda b, c: (b * nc + c, 0)
    fixed = lambda b, c: (0, 0)
    return pl.pallas_call(
        functools.partial(_mlstm_kernel, dk=dk, dv=dv, scale=dk ** -0.5),
        grid=(batch, nc),
        in_specs=[pl.BlockSpec((L, heads * dk), tok),
                  pl.BlockSpec((L, heads * dk), lambda b, c: (b * nc + c, 1)),
                  pl.BlockSpec((L, d), lambda b, c: (b * nc + c, 1)),
                  pl.BlockSpec((L, d), tok),
                  pl.BlockSpec((L, LANES), lambda b, c: (b * nc + c, d // LANES)),
                  pl.BlockSpec((1, LANES), fixed),
                  pl.BlockSpec((1, d), fixed)],
        out_specs=pl.BlockSpec((L, d), tok),
        out_shape=jax.ShapeDtypeStruct((n, d), BF16),
        scratch_shapes=[pltpu.VMEM((heads, dk, dv + LANES), F32),
                        pltpu.VMEM((heads, LANES), F32)],
        compiler_params=_params("arbitrary", "arbitrary"),
        name="mlstm",
    )(qkv, qkv, qkv, og, og, bias, norm_g.reshape(1, d))


def _sgu_kernel(u_ref, v_ref, g_ref, b_ref, ws_ref, bs_ref, y_ref):
    L, width = u_ref.shape
    dg = width // SG_GROUPS
    vn = _layer_norm(v_ref[...], g_ref[...], b_ref[...]).astype(BF16)
    row = lax.broadcasted_iota(jnp.int32, (L, L), 0)
    col = lax.broadcasted_iota(jnp.int32, (L, L), 1)
    for g in range(SG_GROUPS):
        ws = jnp.where(col <= row, ws_ref[g], 0.0).astype(BF16)
        mixed = _dot(ws, vn[:, g * dg:(g + 1) * dg]) + bs_ref[:, g:g + 1]
        y_ref[:, g * dg:(g + 1) * dg] = (u_ref[:, g * dg:(g + 1) * dg] * mixed).astype(y_ref.dtype)


def _sgu(uv, norm_g, norm_b, w_s, b_s):
    n, two_w = uv.shape
    width = two_w // 2
    L = CHUNK
    bs_t = jnp.zeros((L, LANES), F32).at[:, :SG_GROUPS].set(b_s.T)
    fixed = lambda i: (0, 0)
    return pl.pallas_call(
        _sgu_kernel,
        grid=(n // L,),
        in_specs=[pl.BlockSpec((L, width), lambda i: (i, 0)),
                  pl.BlockSpec((L, width), lambda i: (i, 1)),
                  pl.BlockSpec((1, width), fixed), pl.BlockSpec((1, width), fixed),
                  pl.BlockSpec((SG_GROUPS, L, L), lambda i: (0, 0, 0)),
                  pl.BlockSpec((L, LANES), fixed)],
        out_specs=pl.BlockSpec((L, width), lambda i: (i, 0)),
        out_shape=jax.ShapeDtypeStruct((n, width), BF16),
        compiler_params=_params("arbitrary"),
        name="sgu",
    )(uv, uv, norm_g.reshape(1, width), norm_b.reshape(1, width), w_s, bs_t)


def _extract_top(x, count):
    rows = x.shape[0]
    iota = lax.broadcasted_iota(jnp.int32, x.shape, 0)
    rank = jnp.full(x.shape, NOT_RANKED, F32)
    vals = []
    for k in range(count):
        mx = jnp.max(x, axis=0, keepdims=True)
        first = jnp.min(jnp.where(x == mx, iota, rows), axis=0, keepdims=True)
        sel = iota == first
        vals.append(mx)
        rank = jnp.where(sel, float(k + 1), rank)
        x = jnp.where(sel, -jnp.inf, x)
    return vals, rank


def _peer_route_kernel(q_ref, keys_ref, r2_ref, c1_ref, e1_ref, e2_ref):
    K = PEER_TOPK
    half = keys_ref.shape[2]
    s1 = _dot_nt(keys_ref[0, 0], q_ref[:, :half])
    s2 = _dot_nt(keys_ref[0, 1], q_ref[:, half:])
    a, rank1 = _extract_top(s1, K)
    b, rank2 = _extract_top(s2, K)
    T = s1.shape[1]

    b_all = jnp.concatenate(b, axis=0)
    sub = lax.broadcasted_iota(jnp.int32, (8, T), 0)
    blocks = [a[0] + b_all]
    for p in range(1, K):
        blocks.append(jnp.where(sub < K // (p + 1), a[p] + b_all[:8], -jnp.inf))
    cand = jnp.concatenate(blocks, axis=0)
    best, order = _extract_top(cand, K)
    chosen = (order < NOT_RANKED).astype(F32)
    counts = [jnp.sum(chosen[:K], axis=0, keepdims=True)]
    for p in range(1, K):
        counts.append(jnp.sum(chosen[K + 8 * (p - 1):K + 8 * p], axis=0, keepdims=True))

    z = jnp.zeros_like(best[0])
    for k in range(K):
        z = z + jnp.exp(best[k] - best[0])
    c1 = jnp.zeros_like(s1)
    for p in range(K):
        c1 = jnp.where(rank1 == float(p + 1), counts[p], c1)
    r2_ref[0] = rank2
    c1_ref[0] = c1
    e1_ref[0] = jnp.exp(s1 - a[0]) / z
    e2_ref[0] = jnp.exp(s2 - b[0])


def _peer_route(q, keys, tt=LANES):
    n = q.shape[0]
    heads, _, nk, half = keys.shape
    out = jax.ShapeDtypeStruct((heads, nk, n), F32)
    spec = pl.BlockSpec((1, nk, tt), lambda i, h: (h, 0, i))
    return pl.pallas_call(
        _peer_route_kernel,
        grid=(n // tt, heads),
        in_specs=[pl.BlockSpec((tt, 2 * half), lambda i, h: (i, h)),
                  pl.BlockSpec((1, 2, nk, half), lambda i, h: (h, 0, 0, 0))],
        out_specs=[spec] * 4,
        out_shape=[out] * 4,
        compiler_params=_params("arbitrary", "arbitrary"),
        name="peer_route",
    )(q, keys)


def _peer_expert_kernel(x_ref, u_ref, vt_ref, r2_ref, e2_ref, c1_ref, e1_ref, o_ref, p_ref, *, lane_chunk):
    eb = pl.program_id(1)
    nk = PEER_KEYS
    rows_i = u_ref.shape[0] // nk
    tb = x_ref.shape[0]

    @pl.when(eb == 0)
    def _():
        o_ref[...] = jnp.zeros_like(o_ref)

    act = _gelu(_dot_nt(u_ref[...], x_ref[...]))
    for ii in range(rows_i):
        for t0 in range(0, tb, lane_chunk):
            ts = slice(t0, t0 + lane_chunk)
            gate = jnp.zeros((nk, lane_chunk), F32)
            for h in range(PEER_HEADS):
                c1 = c1_ref[h, ii:ii + 1, ts]
                e1 = e1_ref[h, ii:ii + 1, ts]
                gate = gate + jnp.where(r2_ref[h, :, ts] <= c1, e2_ref[h, :, ts], 0.0) * e1
            p_ref[ii * nk:(ii + 1) * nk, ts] = (gate * act[ii * nk:(ii + 1) * nk, ts]).astype(BF16)
    o_ref[...] += _dot(vt_ref[...], p_ref[...])


def _peer_experts(x, u, vt, r2, e2, c1, e1, tb=512, eb=1024, lane_chunk=256):
    n, d = x.shape
    ne = u.shape[0]
    heads, nk, _ = r2.shape
    tb = min(tb, n)
    rows_i = eb // nk
    tok = lambda t, e: (0, 0, t)
    return pl.pallas_call(
        functools.partial(_peer_expert_kernel, lane_chunk=min(lane_chunk, tb)),
        grid=(n // tb, ne // eb),
        in_specs=[pl.BlockSpec((tb, d), lambda t, e: (t, 0)),
                  pl.BlockSpec((eb, d), lambda t, e: (e, 0)),
                  pl.BlockSpec((d, eb), lambda t, e: (0, e)),
                  pl.BlockSpec((heads, nk, tb), tok),
                  pl.BlockSpec((heads, nk, tb), tok),
                  pl.BlockSpec((heads, rows_i, tb), lambda t, e: (0, e, t)),
                  pl.BlockSpec((heads, rows_i, tb), lambda t, e: (0, e, t))],
        out_specs=pl.BlockSpec((d, tb), lambda t, e: (0, t)),
        out_shape=jax.ShapeDtypeStruct((d, n), F32),
        scratch_shapes=[pltpu.VMEM((eb, tb), BF16)],
        compiler_params=_params("arbitrary", "arbitrary"),
        name="peer_experts",
    )(x, u, vt, r2, e2, c1, e1)


def _residual_ln_t_kernel(h_ref, yt_ref, g_ref, b_ref, o_ref, ob_ref):
    out = _layer_norm(ALPHA * h_ref[...] + yt_ref[...].T, g_ref[...], b_ref[...])
    o_ref[...] = out
    ob_ref[...] = out.astype(BF16)


def _residual_ln_t(h, yt, g, b, tm=256):
    n, d = h.shape
    tm = min(tm, n)
    row = lambda i: (i, 0)
    fixed = lambda i: (0, 0)
    return pl.pallas_call(
        _residual_ln_t_kernel,
        grid=(n // tm,),
        in_specs=[pl.BlockSpec((tm, d), row), pl.BlockSpec((d, tm), lambda i: (0, i)),
                  pl.BlockSpec((1, d), fixed), pl.BlockSpec((1, d), fixed)],
        out_specs=[pl.BlockSpec((tm, d), row), pl.BlockSpec((tm, d), row)],
        out_shape=[jax.ShapeDtypeStruct((n, d), F32), jax.ShapeDtypeStruct((n, d), BF16)],
        compiler_params=_params("arbitrary"),
        name="residual_ln_t",
    )(h, yt, g.reshape(1, d), b.reshape(1, d))


def _peer_layer(h, hb, w_q, sub_keys, u_tab, v_tab, ln_g, ln_b):
    q = _matmul(hb, w_q.astype(BF16), BF16, name="peer_query")
    r2, c1, e1, e2 = _peer_route(q, sub_keys.astype(BF16))
    yt = _peer_experts(hb, u_tab.astype(BF16), v_tab.T.astype(BF16), r2, e2, c1, e1)
    return _residual_ln_t(h, yt, ln_g, ln_b)


def _sb_mixer(hb, batch, seq, w_in, w_out):
    qkv = _matmul(hb, w_in.astype(BF16), BF16, name="sb_in")
    return _sb_attention(qkv, batch, seq), w_out


def _ml_mixer(hb, batch, seq, w_in, b_gates, norm_g, w_out):
    heads = ML_HEADS
    d_gate = w_out.shape[0]
    n_qkv = w_in.shape[1] - d_gate - 2 * heads
    qkv = _matmul(hb, w_in[:, :n_qkv].astype(BF16), BF16, name="ml_in_qkv")
    w_og = jnp.pad(w_in[:, n_qkv:], ((0, 0), (0, LANES - 2 * heads))).astype(BF16)
    og = _matmul(hb, w_og, F32, tn=w_og.shape[1], name="ml_in_gates")
    bias = jnp.pad(b_gates.astype(F32), (0, LANES - 2 * heads)).reshape(1, LANES)
    return _mlstm(qkv, og, bias, norm_g, batch, seq), w_out


def _sg_mixer(hb, batch, seq, w_in, norm_g, norm_b, w_s, b_s, w_out):
    uv = _matmul(hb, w_in.astype(BF16), F32, act="gelu", name="sg_in")
    return _sgu(uv, norm_g, norm_b, w_s, b_s), w_out


def kernel(x, l0_sb_w_in, l0_sb_w_out, l0_ln1_g, l0_ln1_b, l0_peer_w_q, l0_peer_sub_keys, l0_peer_u, l0_peer_v, l0_ln2_g, l0_ln2_b, l1_ml_w_in, l1_ml_b_gates, l1_ml_norm_g, l1_ml_w_out, l1_ln1_g, l1_ln1_b, l1_peer_w_q, l1_peer_sub_keys, l1_peer_u, l1_peer_v, l1_ln2_g, l1_ln2_b, l2_sg_w_in, l2_sg_norm_g, l2_sg_norm_b, l2_sg_w_s, l2_sg_b_s, l2_sg_w_out, l2_ln1_g, l2_ln1_b, l2_peer_w_q, l2_peer_sub_keys, l2_peer_u, l2_peer_v, l2_ln2_g, l2_ln2_b, l3_sb_w_in, l3_sb_w_out, l3_ln1_g, l3_ln1_b, l3_peer_w_q, l3_peer_sub_keys, l3_peer_u, l3_peer_v, l3_ln2_g, l3_ln2_b):
    batch, seq, d = x.shape
    mixers = (
        (_sb_mixer, (l0_sb_w_in, l0_sb_w_out)),
        (_ml_mixer, (l1_ml_w_in, l1_ml_b_gates, l1_ml_norm_g, l1_ml_w_out)),
        (_sg_mixer, (l2_sg_w_in, l2_sg_norm_g, l2_sg_norm_b, l2_sg_w_s, l2_sg_b_s, l2_sg_w_out)),
        (_sb_mixer, (l3_sb_w_in, l3_sb_w_out)),
    )
    norm1 = ((l0_ln1_g, l0_ln1_b), (l1_ln1_g, l1_ln1_b), (l2_ln1_g, l2_ln1_b), (l3_ln1_g, l3_ln1_b))
    peers = (
        (l0_peer_w_q, l0_peer_sub_keys, l0_peer_u, l0_peer_v, l0_ln2_g, l0_ln2_b),
        (l1_peer_w_q, l1_peer_sub_keys, l1_peer_u, l1_peer_v, l1_ln2_g, l1_ln2_b),
        (l2_peer_w_q, l2_peer_sub_keys, l2_peer_u, l2_peer_v, l2_ln2_g, l2_ln2_b),
        (l3_peer_w_q, l3_peer_sub_keys, l3_peer_u, l3_peer_v, l3_ln2_g, l3_ln2_b),
    )
    h = x.reshape(batch * seq, d)
    hb = h.astype(BF16)
    for (mixer, mixer_params), (g1, b1), peer_params in zip(mixers, norm1, peers):
        pre, w_out = mixer(hb, batch, seq, *mixer_params)
        h, hb = _proj_residual_ln(pre, w_out.astype(BF16), h, g1, b1)
        h, hb = _peer_layer(h, hb, *peer_params)
    return h.reshape(batch, seq, d)
```

```python
import functools

import jax
import jax.numpy as jnp
from jax import lax
from jax.experimental import pallas as pl
from jax.experimental.pallas import tpu as pltpu

F32 = jnp.float32
BF16 = jnp.bfloat16

LANES = 128
BF16_SUBLANES = 16
CHUNK = 128
SB_HEADS = 16
ML_HEADS = 8
SG_GROUPS = 8
PEER_HEADS = 8
PEER_KEYS = 128
PEER_TOPK = 16
DEPTH = 4
ALPHA = (2 * DEPTH) ** 0.25
LN_EPS = 1e-5
VMEM_LIMIT_BYTES = 56 * 1024 * 1024
EXP_ZERO_BELOW = -110.0
NOT_RANKED = 99.0


def _params(*sem):
    return pltpu.CompilerParams(dimension_semantics=sem, vmem_limit_bytes=VMEM_LIMIT_BYTES)


def _gelu(x):
    return 0.5 * x * (1.0 + lax.erf(x * 0.7071067811865476))


def _softplus(z):
    return jnp.maximum(z, 0.0) + jnp.log1p(jnp.exp(-jnp.abs(z)))


def _log_sigmoid(z):
    return -_softplus(-z)


def _layer_norm(t, g, b):
    mu = jnp.mean(t, axis=-1, keepdims=True)
    d = t - mu
    var = jnp.mean(d * d, axis=-1, keepdims=True)
    return d * lax.rsqrt(var + LN_EPS) * g + b


def _dot(a, b):
    return jnp.dot(a, b, preferred_element_type=F32)


def _dot_nt(a, b):
    return lax.dot_general(a, b, (((1,), (1,)), ((), ())), preferred_element_type=F32)


def _dot_tn(a, b):
    return lax.dot_general(a, b, (((0,), (0,)), ((), ())), preferred_element_type=F32)


def _mm_kernel(x_ref, w_ref, o_ref, *, act):
    y = _dot(x_ref[...], w_ref[...])
    if act == "gelu":
        y = _gelu(y)
    o_ref[...] = y.astype(o_ref.dtype)


def _matmul(x, w, out_dtype, act=None, tm=512, tn=512, name="matmul"):
    m, k = x.shape
    n = w.shape[1]
    tm, tn = min(tm, m), min(tn, n)
    assert m % tm == 0 and n % tn == 0, (m, n, tm, tn)
    return pl.pallas_call(
        functools.partial(_mm_kernel, act=act),
        grid=(n // tn, m // tm),
        in_specs=[pl.BlockSpec((tm, k), lambda j, i: (i, 0)),
                  pl.BlockSpec((k, tn), lambda j, i: (0, j))],
        out_specs=pl.BlockSpec((tm, tn), lambda j, i: (i, j)),
        out_shape=jax.ShapeDtypeStruct((m, n), out_dtype),
        compiler_params=_params("arbitrary", "arbitrary"),
        name=name,
    )(x, w)


def _proj_ln_kernel(x_ref, w_ref, h_ref, g_ref, b_ref, o_ref, ob_ref):
    y = _dot(x_ref[...], w_ref[...])
    out = _layer_norm(ALPHA * h_ref[...] + y, g_ref[...], b_ref[...])
    o_ref[...] = out
    ob_ref[...] = out.astype(BF16)


def _proj_residual_ln(x, w, h, g, b, tm=256):
    m, k = x.shape
    d = w.shape[1]
    tm = min(tm, m)
    assert m % tm == 0
    row = lambda i: (i, 0)
    fixed = lambda i: (0, 0)
    return pl.pallas_call(
        _proj_ln_kernel,
        grid=(m // tm,),
        in_specs=[pl.BlockSpec((tm, k), row), pl.BlockSpec((k, d), fixed),
                  pl.BlockSpec((tm, d), row), pl.BlockSpec((1, d), fixed),
                  pl.BlockSpec((1, d), fixed)],
        out_specs=[pl.BlockSpec((tm, d), row), pl.BlockSpec((tm, d), row)],
        out_shape=[jax.ShapeDtypeStruct((m, d), F32), jax.ShapeDtypeStruct((m, d), BF16)],
        compiler_params=_params("arbitrary"),
        name="proj_residual_ln",
    )(x, w, h, g.reshape(1, d), b.reshape(1, d))


def _sb_kernel(q_ref, k_ref, v_ref, o_ref, *, blk, dh, scale):
    qi = pl.program_id(2)
    heads = q_ref.shape[1] // dh
    row = lax.broadcasted_iota(jnp.int32, (blk, blk), 0)
    col = lax.broadcasted_iota(jnp.int32, (blk, blk), 1)
    suffix = jnp.concatenate(
        [(row > col).astype(BF16), jnp.ones((blk, LANES), BF16)], axis=1)
    strict = col < row

    def key_block(kb, h, carry_sum, acc, diagonal):
        start = pl.multiple_of(kb * blk, blk)
        cols = slice(h * dh, (h + 1) * dh)
        k = k_ref[pl.ds(start, blk), cols]
        v = v_ref[pl.ds(start, blk), cols]
        z = _dot_nt(q_ref[:, cols], k) * scale
        sp = _softplus(z)
        log_keep = jnp.where(strict, -sp, 0.0) if diagonal else -sp
        hi = log_keep.astype(BF16)
        lo = (log_keep - hi.astype(F32)).astype(BF16)
        sums = _dot(hi, suffix) + _dot(lo, suffix)
        between = sums[:, :blk] + jnp.concatenate([carry_sum] * (blk // LANES), axis=1)
        w = jnp.exp(z - sp + between)
        if diagonal:
            w = jnp.where(strict, w, 0.0)
        return carry_sum + sums[:, blk:], acc + _dot(w.astype(BF16), v)

    def all_heads(kb, sums, accs, diagonal):
        out = [key_block(kb, h, sums[h], accs[h], diagonal) for h in range(heads)]
        return tuple(o[0] for o in out), tuple(o[1] for o in out)

    def largest(sums):
        return jnp.max(functools.reduce(jnp.maximum, sums))

    sums, accs = all_heads(qi, (jnp.zeros((blk, LANES), F32),) * heads,
                           (jnp.zeros((blk, dh), F32),) * heads, True)

    def cond(c):
        return jnp.logical_and(c[0] >= 0, c[1] > EXP_ZERO_BELOW)

    def body(c):
        sums, accs = all_heads(c[0], c[2], c[3], False)
        return c[0] - 1, largest(sums), sums, accs

    accs = lax.while_loop(cond, body, (qi - 1, largest(sums), sums, accs))[3]
    for h in range(heads):
        o_ref[:, h * dh:(h + 1) * dh] = accs[h].astype(o_ref.dtype)


def _sb_attention(qkv, batch, seq, blk=CHUNK, heads_per_step=4):
    n, three_d = qkv.shape
    d = three_d // 3
    dh = d // SB_HEADS
    groups = SB_HEADS // heads_per_step
    width = heads_per_step * dh
    nq = seq // blk
    return pl.pallas_call(
        functools.partial(_sb_kernel, blk=blk, dh=dh, scale=dh ** -0.5),
        grid=(batch, groups, nq),
        in_specs=[pl.BlockSpec((blk, width), lambda b, g, i: (b * nq + i, g)),
                  pl.BlockSpec((seq, width), lambda b, g, i: (b, groups + g)),
                  pl.BlockSpec((seq, width), lambda b, g, i: (b, 2 * groups + g))],
        out_specs=pl.BlockSpec((blk, width), lambda b, g, i: (b * nq + i, g)),
        out_shape=jax.ShapeDtypeStruct((n, d), BF16),
        compiler_params=_params("arbitrary", "arbitrary", "arbitrary"),
        name="sb_attention",
    )(qkv, qkv, qkv)


def _mlstm_kernel(q_ref, k_ref, v_ref, o_ref, g_ref, bias_ref, ng_ref, y_ref, c_ref, m_ref,
                  *, dk, dv, scale):
    L = q_ref.shape[0]
    heads = ML_HEADS

    @pl.when(pl.program_id(1) == 0)
    def _():
        c_ref[...] = jnp.zeros_like(c_ref)
        m_ref[...] = jnp.zeros_like(m_ref)

    g = g_ref[...] + bias_ref[...]
    log_f = _log_sigmoid(g)
    row = lax.broadcasted_iota(jnp.int32, (L, L), 0)
    col = lax.broadcasted_iota(jnp.int32, (L, L), 1)
    causal = col <= row
    b_cols = jnp.dot(causal.astype(F32), log_f, precision=lax.Precision.HIGHEST,
                     preferred_element_type=F32)
    b_rows = b_cols.T
    g_rows = g.T
    ones_col = (lax.broadcasted_iota(jnp.int32, (L, LANES), 1) == 0).astype(BF16)

    for h in range(heads):
        q = q_ref[:, h * dk:(h + 1) * dk]
        k = k_ref[:, h * dk:(h + 1) * dk]
        v_ext = jnp.concatenate([v_ref[:, h * dv:(h + 1) * dv], ones_col], axis=1)
        li_col = g[:, h:h + 1]
        li_row = g_rows[h:h + 1, :]
        b_col = b_cols[:, heads + h:heads + h + 1]
        b_row = b_rows[heads + h:heads + h + 1, :]
        m_prev = m_ref[h:h + 1, 0:1]
        c_ext = c_ref[h]

        d_intra = jnp.where(causal, b_col - b_row + li_row, -jnp.inf)
        d_inter = b_col + m_prev
        m_t = jnp.maximum(d_inter, jnp.max(d_intra, axis=-1, keepdims=True))
        w_intra = jnp.exp(d_intra - m_t)
        w_inter = jnp.exp(d_inter - m_t)
        s = _dot_nt(q, k) * scale * w_intra
        num = _dot(s.astype(BF16), v_ext) + w_inter * (_dot(q, c_ext.astype(BF16)) * scale)
        den = num[:, dv:dv + 1]
        hid = num[:, :dv] / jnp.maximum(jnp.abs(den), jnp.exp(-m_t))

        b_last = b_col[L - 1:L, :]
        m_new = jnp.maximum(b_last + m_prev,
                            jnp.max(b_last - b_row + li_row, axis=-1, keepdims=True))
        decay = jnp.exp(b_last + m_prev - m_new)
        w_state = jnp.exp(b_last - b_col + li_col - m_new)
        kw = (k.astype(F32) * w_state).astype(BF16)
        c_ref[h] = decay * c_ext + _dot_tn(kw, v_ext)
        m_ref[h:h + 1, :] = jnp.broadcast_to(m_new, (1, LANES))

        hid = hid * lax.rsqrt(jnp.mean(hid * hid, axis=-1, keepdims=True) + LN_EPS)
        hid = hid * ng_ref[:, h * dv:(h + 1) * dv]
        gate = jax.nn.sigmoid(o_ref[:, h * dv:(h + 1) * dv])
        y_ref[:, h * dv:(h + 1) * dv] = (gate * hid).astype(y_ref.dtype)


def _mlstm(qkv, og, bias, norm_g, batch, seq):
    n = qkv.shape[0]
    heads = ML_HEADS
    d = og.shape[1] - LANES
    dv = d // heads
    dk = dv // 2
    L = CHUNK
    nc = seq // L
    tok = lambda b, c: (b * nc + c, 0)
    fixed = lambda b, c: (0, 0)
    return pl.pallas_call(
        functools.partial(_mlstm_kernel, dk=dk, dv=dv, scale=dk ** -0.5),
        grid=(batch, nc),
        in_specs=[pl.BlockSpec((L, heads * dk), tok),
                  pl.BlockSpec((L, heads * dk), lambda b, c: (b * nc + c, 1)),
                  pl.BlockSpec((L, d), lambda b, c: (b * nc + c, 1)),
                  pl.BlockSpec((L, d), tok),
                  pl.BlockSpec((L, LANES), lambda b, c: (b * nc + c, d // LANES)),
                  pl.BlockSpec((1, LANES), fixed),
                  pl.BlockSpec((1, d), fixed)],
        out_specs=pl.BlockSpec((L, d), tok),
        out_shape=jax.ShapeDtypeStruct((n, d), BF16),
        scratch_shapes=[pltpu.VMEM((heads, dk, dv + LANES), F32),
                        pltpu.VMEM((heads, LANES), F32)],
        compiler_params=_params("arbitrary", "arbitrary"),
        name="mlstm",
    )(qkv, qkv, qkv, og, og, bias, norm_g.reshape(1, d))


def _sgu_kernel(u_ref, v_ref, g_ref, b_ref, ws_ref, bs_ref, y_ref):
    L, width = u_ref.shape
    dg = width // SG_GROUPS
    vn = _layer_norm(v_ref[...], g_ref[...], b_ref[...]).astype(BF16)
    row = lax.broadcasted_iota(jnp.int32, (L, L), 0)
    col = lax.broadcasted_iota(jnp.int32, (L, L), 1)
    for g in range(SG_GROUPS):
        ws = jnp.where(col <= row, ws_ref[g], 0.0).astype(BF16)
        mixed = _dot(ws, vn[:, g * dg:(g + 1) * dg]) + bs_ref[:, g:g + 1]
        y_ref[:, g * dg:(g + 1) * dg] = (u_ref[:, g * dg:(g + 1) * dg] * mixed).astype(y_ref.dtype)


def _sgu(uv, norm_g, norm_b, w_s, b_s):
    n, two_w = uv.shape
    width = two_w // 2
    L = CHUNK
    bs_t = jnp.zeros((L, LANES), F32).at[:, :SG_GROUPS].set(b_s.T)
    fixed = lambda i: (0, 0)
    return pl.pallas_call(
        _sgu_kernel,
        grid=(n // L,),
        in_specs=[pl.BlockSpec((L, width), lambda i: (i, 0)),
                  pl.BlockSpec((L, width), lambda i: (i, 1)),
                  pl.BlockSpec((1, width), fixed), pl.BlockSpec((1, width), fixed),
                  pl.BlockSpec((SG_GROUPS, L, L), lambda i: (0, 0, 0)),
                  pl.BlockSpec((L, LANES), fixed)],
        out_specs=pl.BlockSpec((L, width), lambda i: (i, 0)),
        out_shape=jax.ShapeDtypeStruct((n, width), BF16),
        compiler_params=_params("arbitrary"),
        name="sgu",
    )(uv, uv, norm_g.reshape(1, width), norm_b.reshape(1, width), w_s, bs_t)


def _extract_top(x, count, first_only):
    rows = x.shape[0]
    iota = lax.broadcasted_iota(jnp.int32, x.shape, 0)
    rank = jnp.full(x.shape, NOT_RANKED, F32)
    vals = []
    for k in range(count):
        mx = jnp.max(x, axis=0, keepdims=True)
        sel = x == mx
        if first_only:
            sel = iota == jnp.min(jnp.where(sel, iota, rows), axis=0, keepdims=True)
        vals.append(mx)
        rank = jnp.where(sel, float(k + 1), rank)
        x = jnp.where(sel, -jnp.inf, x)
    return vals, rank


def _count_ranked(rank):
    return jnp.sum((rank < NOT_RANKED).astype(F32), axis=0, keepdims=True)


def _route_one_head(s1, s2, first_only):
    K = PEER_TOPK
    T = s1.shape[1]
    a, rank1 = _extract_top(s1, K, first_only)
    b, rank2 = _extract_top(s2, K, first_only)

    b_all = jnp.concatenate(b, axis=0)
    sub = lax.broadcasted_iota(jnp.int32, (8, T), 0)
    blocks = [a[0] + b_all]
    for p in range(1, K):
        blocks.append(jnp.where(sub < K // (p + 1), a[p] + b_all[:8], -jnp.inf))
    cand = jnp.concatenate(blocks, axis=0)
    best, order = _extract_top(cand, K, first_only)
    chosen = (order < NOT_RANKED).astype(F32)
    counts = [jnp.sum(chosen[:K], axis=0, keepdims=True)]
    for p in range(1, K):
        counts.append(jnp.sum(chosen[K + 8 * (p - 1):K + 8 * p], axis=0, keepdims=True))
    ranked = jnp.maximum(jnp.maximum(_count_ranked(rank1), _count_ranked(rank2)),
                         functools.reduce(jnp.add, counts))

    z = jnp.zeros_like(best[0])
    for k in range(K):
        z = z + jnp.exp(best[k] - best[0])
    c1 = jnp.zeros_like(s1)
    for p in range(K):
        c1 = jnp.where(rank1 == float(p + 1), counts[p], c1)
    return rank2, c1, jnp.exp(s1 - a[0]) / z, jnp.exp(s2 - b[0]), ranked


def _peer_route_kernel(q_ref, keys_ref, r2_ref, c1_ref, e1_ref, e2_ref):
    half = keys_ref.shape[3]

    def route(first_only):
        ranked = []
        for h in range(keys_ref.shape[0]):
            s1 = _dot_nt(keys_ref[h, 0], q_ref[:, 2 * h * half:(2 * h + 1) * half])
            s2 = _dot_nt(keys_ref[h, 1], q_ref[:, (2 * h + 1) * half:(2 * h + 2) * half])
            r2, c1_ref[h], e1_ref[h], e2, n = _route_one_head(s1, s2, first_only)
            r2_ref[h] = r2.astype(r2_ref.dtype)
            e2_ref[h] = e2.astype(e2_ref.dtype)
            ranked.append(n)
        return jnp.max(functools.reduce(jnp.maximum, ranked))

    most_ranked = route(False)

    @pl.when(most_ranked > PEER_TOPK)
    def _():
        route(True)


def _peer_route(q, keys, tt=LANES, heads_per_step=2):
    n = q.shape[0]
    heads, _, nk, half = keys.shape
    hp = heads_per_step
    out = lambda dtype: jax.ShapeDtypeStruct((heads, nk, n), dtype)
    spec = pl.BlockSpec((hp, nk, tt), lambda i, g: (g, 0, i))
    return pl.pallas_call(
        _peer_route_kernel,
        grid=(n // tt, heads // hp),
        in_specs=[pl.BlockSpec((tt, 2 * half * hp), lambda i, g: (i, g)),
                  pl.BlockSpec((hp, 2, nk, half), lambda i, g: (g, 0, 0, 0))],
        out_specs=[spec] * 4,
        out_shape=[out(BF16), out(F32), out(F32), out(BF16)],
        compiler_params=_params("arbitrary", "arbitrary"),
        name="peer_route",
    )(q, keys)


def _peer_expert_kernel(x_ref, u_ref, vta_ref, vtb_ref, r2_ref, e2_ref, c1_ref, e1_ref, o_ref,
                        pa_ref, pb_ref, *, lane_chunk, sub_experts):
    e = pl.program_id(1)
    last = pl.num_programs(1) - 1
    nk = PEER_KEYS
    tb = x_ref.shape[0]
    half = u_ref.shape[0] // 2
    n_sub = half // sub_experts
    d_rows = vta_ref.shape[0] // n_sub

    def rows_to_tile(ref, h, ii, ts):
        row = jnp.broadcast_to(ref[h, ii:ii + 1, ts], (BF16_SUBLANES, lane_chunk)).astype(BF16)
        return jnp.concatenate([row] * (nk // BF16_SUBLANES), axis=0)

    def evaluate(p_ref, first, c):
        rows = slice(first + c * sub_experts, first + (c + 1) * sub_experts)
        act = _gelu(_dot_nt(u_ref[rows, :], x_ref[...])).astype(BF16)
        for r0 in range(0, sub_experts, nk):
            ii = (rows.start + r0) // nk
            for t0 in range(0, tb, lane_chunk):
                ts = slice(t0, t0 + lane_chunk)
                gate = jnp.zeros((nk, lane_chunk), BF16)
                for h in range(PEER_HEADS):
                    picked = r2_ref[h, :, ts] <= rows_to_tile(c1_ref, h, ii, ts)
                    gate = gate + jnp.where(picked, e2_ref[h, :, ts], 0.0) * rows_to_tile(e1_ref, h, ii, ts)
                p_ref[c * sub_experts + r0:c * sub_experts + r0 + nk, ts] = gate * act[r0:r0 + nk, ts]

    def apply(vt_ref, p_ref, c):
        rows = slice(c * d_rows, (c + 1) * d_rows)
        o_ref[rows, :] += _dot(vt_ref[rows, :], p_ref[...])

    @pl.when(e == 0)
    def _():
        o_ref[...] = jnp.zeros_like(o_ref)
        pb_ref[...] = jnp.zeros_like(pb_ref)

    @pl.when(e < last)
    def _():
        for c in range(n_sub):
            apply(vta_ref, pb_ref, c)
            evaluate(pa_ref, 0, c)
        for c in range(n_sub):
            apply(vtb_ref, pa_ref, c)
            evaluate(pb_ref, half, c)

    @pl.when(e == last)
    def _():
        for c in range(n_sub):
            apply(vta_ref, pb_ref, c)


def _peer_experts(x, u, vt, r2, e2, c1, e1, tb=512, eb=1024, lane_chunk=256, sub_experts=256):
    n, d = x.shape
    ne = u.shape[0] // eb
    heads, nk, _ = r2.shape
    tb = min(tb, n)
    rows_i = eb // nk
    half = eb // 2
    tok = lambda t, e: (0, 0, t)
    cur = lambda e: jnp.minimum(e, ne - 1)
    return pl.pallas_call(
        functools.partial(_peer_expert_kernel, lane_chunk=min(lane_chunk, tb), sub_experts=sub_experts),
        grid=(n // tb, ne + 1),
        in_specs=[pl.BlockSpec((tb, d), lambda t, e: (t, 0)),
                  pl.BlockSpec((eb, d), lambda t, e: (cur(e), 0)),
                  pl.BlockSpec((d, half), lambda t, e: (0, jnp.maximum(2 * e - 1, 0))),
                  pl.BlockSpec((d, half), lambda t, e: (0, 2 * cur(e))),
                  pl.BlockSpec((heads, nk, tb), tok),
                  pl.BlockSpec((heads, nk, tb), tok),
                  pl.BlockSpec((heads, rows_i, tb), lambda t, e: (0, cur(e), t)),
                  pl.BlockSpec((heads, rows_i, tb), lambda t, e: (0, cur(e), t))],
        out_specs=pl.BlockSpec((d, tb), lambda t, e: (0, t)),
        out_shape=jax.ShapeDtypeStruct((d, n), F32),
        scratch_shapes=[pltpu.VMEM((half, tb), BF16), pltpu.VMEM((half, tb), BF16)],
        compiler_params=_params("arbitrary", "arbitrary"),
        name="peer_experts",
    )(x, u, vt, vt, r2, e2, c1, e1)


def _residual_ln_t_kernel(h_ref, yt_ref, g_ref, b_ref, o_ref, ob_ref):
    out = _layer_norm(ALPHA * h_ref[...] + yt_ref[...].T, g_ref[...], b_ref[...])
    o_ref[...] = out
    ob_ref[...] = out.astype(BF16)


def _residual_ln_t(h, yt, g, b, tm=256):
    n, d = h.shape
    tm = min(tm, n)
    row = lambda i: (i, 0)
    fixed = lambda i: (0, 0)
    return pl.pallas_call(
        _residual_ln_t_kernel,
        grid=(n // tm,),
        in_specs=[pl.BlockSpec((tm, d), row), pl.BlockSpec((d, tm), lambda i: (0, i)),
                  pl.BlockSpec((1, d), fixed), pl.BlockSpec((1, d), fixed)],
        out_specs=[pl.BlockSpec((tm, d), row), pl.BlockSpec((tm, d), row)],
        out_shape=[jax.ShapeDtypeStruct((n, d), F32), jax.ShapeDtypeStruct((n, d), BF16)],
        compiler_params=_params("arbitrary"),
        name="residual_ln_t",
    )(h, yt, g.reshape(1, d), b.reshape(1, d))


def _peer_layer(h, hb, w_q, sub_keys, u_tab, v_tab, ln_g, ln_b):
    q = _matmul(hb, w_q.astype(BF16), BF16, name="peer_query")
    r2, c1, e1, e2 = _peer_route(q, sub_keys.astype(BF16))
    yt = _peer_experts(hb, u_tab.astype(BF16), v_tab.T.astype(BF16), r2, e2, c1, e1)
    return _residual_ln_t(h, yt, ln_g, ln_b)


def _sb_mixer(hb, batch, seq, w_in, w_out):
    qkv = _matmul(hb, w_in.astype(BF16), BF16, name="sb_in")
    return _sb_attention(qkv, batch, seq), w_out


def _ml_mixer(hb, batch, seq, w_in, b_gates, norm_g, w_out):
    heads = ML_HEADS
    d_gate = w_out.shape[0]
    n_qkv = w_in.shape[1] - d_gate - 2 * heads
    qkv = _matmul(hb, w_in[:, :n_qkv].astype(BF16), BF16, name="ml_in_qkv")
    w_og = jnp.pad(w_in[:, n_qkv:], ((0, 0), (0, LANES - 2 * heads))).astype(BF16)
    og = _matmul(hb, w_og, F32, tn=w_og.shape[1], name="ml_in_gates")
    bias = jnp.pad(b_gates.astype(F32), (0, LANES - 2 * heads)).reshape(1, LANES)
    return _mlstm(qkv, og, bias, norm_g, batch, seq), w_out


def _sg_mixer(hb, batch, seq, w_in, norm_g, norm_b, w_s, b_s, w_out):
    uv = _matmul(hb, w_in.astype(BF16), F32, act="gelu", name="sg_in")
    return _sgu(uv, norm_g, norm_b, w_s, b_s), w_out


def kernel(x, l0_sb_w_in, l0_sb_w_out, l0_ln1_g, l0_ln1_b, l0_peer_w_q, l0_peer_sub_keys, l0_peer_u, l0_peer_v, l0_ln2_g, l0_ln2_b, l1_ml_w_in, l1_ml_b_gates, l1_ml_norm_g, l1_ml_w_out, l1_ln1_g, l1_ln1_b, l1_peer_w_q, l1_peer_sub_keys, l1_peer_u, l1_peer_v, l1_ln2_g, l1_ln2_b, l2_sg_w_in, l2_sg_norm_g, l2_sg_norm_b, l2_sg_w_s, l2_sg_b_s, l2_sg_w_out, l2_ln1_g, l2_ln1_b, l2_peer_w_q, l2_peer_sub_keys, l2_peer_u, l2_peer_v, l2_ln2_g, l2_ln2_b, l3_sb_w_in, l3_sb_w_out, l3_ln1_g, l3_ln1_b, l3_peer_w_q, l3_peer_sub_keys, l3_peer_u, l3_peer_v, l3_ln2_g, l3_ln2_b):
    batch, seq, d = x.shape
    mixers = (
        (_sb_mixer, (l0_sb_w_in, l0_sb_w_out)),
        (_ml_mixer, (l1_ml_w_in, l1_ml_b_gates, l1_ml_norm_g, l1_ml_w_out)),
        (_sg_mixer, (l2_sg_w_in, l2_sg_norm_g, l2_sg_norm_b, l2_sg_w_s, l2_sg_b_s, l2_sg_w_out)),
        (_sb_mixer, (l3_sb_w_in, l3_sb_w_out)),
    )
    norm1 = ((l0_ln1_g, l0_ln1_b), (l1_ln1_g, l1_ln1_b), (l2_ln1_g, l2_ln1_b), (l3_ln1_g, l3_ln1_b))
    peers = (
        (l0_peer_w_q, l0_peer_sub_keys, l0_peer_u, l0_peer_v, l0_ln2_g, l0_ln2_b),
        (l1_peer_w_q, l1_peer_sub_keys, l1_peer_u, l1_peer_v, l1_ln2_g, l1_ln2_b),
        (l2_peer_w_q, l2_peer_sub_keys, l2_peer_u, l2_peer_v, l2_ln2_g, l2_ln2_b),
        (l3_peer_w_q, l3_peer_sub_keys, l3_peer_u, l3_peer_v, l3_ln2_g, l3_ln2_b),
    )
    h = x.reshape(batch * seq, d)
    hb = h.astype(BF16)
    for (mixer, mixer_params), (g1, b1), peer_params in zip(mixers, norm1, peers):
        pre, w_out = mixer(hb, batch, seq, *mixer_params)
        h, hb = _proj_residual_ln(pre, w_out.astype(BF16), h, g1, b1)
        h, hb = _peer_layer(h, hb, *peer_params)
    return h.reshape(batch, seq, d)
```

```python
import functools

import jax
import jax.numpy as jnp
from jax import lax
from jax.experimental import pallas as pl
from jax.experimental.pallas import tpu as pltpu

F32 = jnp.float32
BF16 = jnp.bfloat16

LANES = 128
BF16_SUBLANES = 16
CHUNK = 128
SB_HEADS = 16
ML_HEADS = 8
SG_GROUPS = 8
PEER_HEADS = 8
PEER_KEYS = 128
PEER_TOPK = 16
DEPTH = 4
ALPHA = (2 * DEPTH) ** 0.25
LN_EPS = 1e-5
VMEM_LIMIT_BYTES = 56 * 1024 * 1024
EXP_ZERO_BELOW = -110.0
NOT_RANKED = 99.0


def _params(*sem):
    return pltpu.CompilerParams(dimension_semantics=sem, vmem_limit_bytes=VMEM_LIMIT_BYTES)


def _gelu(x):
    return 0.5 * x * (1.0 + lax.erf(x * 0.7071067811865476))


def _softplus(z):
    return jnp.maximum(z, 0.0) + jnp.log1p(jnp.exp(-jnp.abs(z)))


def _log_sigmoid(z):
    return -_softplus(-z)


def _layer_norm(t, g, b):
    mu = jnp.mean(t, axis=-1, keepdims=True)
    d = t - mu
    var = jnp.mean(d * d, axis=-1, keepdims=True)
    return d * lax.rsqrt(var + LN_EPS) * g + b


def _dot(a, b):
    return jnp.dot(a, b, preferred_element_type=F32)


def _dot_nt(a, b):
    return lax.dot_general(a, b, (((1,), (1,)), ((), ())), preferred_element_type=F32)


def _dot_tn(a, b):
    return lax.dot_general(a, b, (((0,), (0,)), ((), ())), preferred_element_type=F32)


def _mm_kernel(x_ref, w_ref, o_ref, *, act):
    y = _dot(x_ref[...], w_ref[...])
    if act == "gelu":
        y = _gelu(y)
    o_ref[...] = y.astype(o_ref.dtype)


def _matmul(x, w, out_dtype, act=None, tm=1024, tn=1024, name="matmul"):
    m, k = x.shape
    n = w.shape[1]
    tm, tn = min(tm, m), min(tn, n)
    assert m % tm == 0 and n % tn == 0, (m, n, tm, tn)
    return pl.pallas_call(
        functools.partial(_mm_kernel, act=act),
        grid=(n // tn, m // tm),
        in_specs=[pl.BlockSpec((tm, k), lambda j, i: (i, 0)),
                  pl.BlockSpec((k, tn), lambda j, i: (0, j))],
        out_specs=pl.BlockSpec((tm, tn), lambda j, i: (i, j)),
        out_shape=jax.ShapeDtypeStruct((m, n), out_dtype),
        compiler_params=_params("arbitrary", "arbitrary"),
        name=name,
    )(x, w)


def _proj_ln_kernel(x_ref, w_ref, h_ref, g_ref, b_ref, o_ref, ob_ref):
    y = _dot(x_ref[...], w_ref[...])
    out = _layer_norm(ALPHA * h_ref[...] + y, g_ref[...], b_ref[...])
    o_ref[...] = out
    ob_ref[...] = out.astype(BF16)


def _proj_residual_ln(x, w, h, g, b, tm=512):
    m, k = x.shape
    d = w.shape[1]
    tm = min(tm, m)
    assert m % tm == 0
    row = lambda i: (i, 0)
    fixed = lambda i: (0, 0)
    return pl.pallas_call(
        _proj_ln_kernel,
        grid=(m // tm,),
        in_specs=[pl.BlockSpec((tm, k), row), pl.BlockSpec((k, d), fixed),
                  pl.BlockSpec((tm, d), row), pl.BlockSpec((1, d), fixed),
                  pl.BlockSpec((1, d), fixed)],
        out_specs=[pl.BlockSpec((tm, d), row), pl.BlockSpec((tm, d), row)],
        out_shape=[jax.ShapeDtypeStruct((m, d), F32), jax.ShapeDtypeStruct((m, d), BF16)],
        compiler_params=_params("arbitrary"),
        name="proj_residual_ln",
    )(x, w, h, g.reshape(1, d), b.reshape(1, d))


def _sb_kernel(q_ref, k_ref, v_ref, o_ref, *, blk, win, dh, scale):
    qi = pl.program_id(2)
    heads = q_ref.shape[1] // dh
    row = lax.broadcasted_iota(jnp.int32, (win, win), 0)
    col = lax.broadcasted_iota(jnp.int32, (win, win), 1)
    suffix = jnp.concatenate(
        [(row > col).astype(BF16), jnp.ones((win, LANES), BF16)], axis=1)
    q_pos = qi * blk + lax.broadcasted_iota(jnp.int32, (blk, win), 0)
    offset = lax.broadcasted_iota(jnp.int32, (blk, win), 1)

    def window(start, valid, h, carry_sum, acc):
        cols = slice(h * dh, (h + 1) * dh)
        k = k_ref[pl.ds(start, win), cols]
        v = v_ref[pl.ds(start, win), cols]
        z = _dot_nt(q_ref[:, cols], k) * scale
        sp = _softplus(z)
        log_keep = jnp.where(valid, -sp, 0.0)
        hi = log_keep.astype(BF16)
        lo = (log_keep - hi.astype(F32)).astype(BF16)
        sums = _dot(hi, suffix) + _dot(lo, suffix)
        between = sums[:, :win] + jnp.concatenate([carry_sum] * (win // LANES), axis=1)
        w = jnp.where(valid, jnp.exp(z - sp + between), 0.0)
        return carry_sum + sums[:, win:], acc + _dot(w.astype(BF16), v)

    def cond(c):
        return jnp.logical_and(c[0] > 0, c[1] > EXP_ZERO_BELOW)

    def body(c):
        end, _, sums, accs = c
        start = pl.multiple_of(jnp.maximum(end - win, 0), blk)
        key_pos = start + offset
        valid = jnp.logical_and(key_pos < q_pos, key_pos < end)
        out = [window(start, valid, h, sums[h], accs[h]) for h in range(heads)]
        sums = tuple(o[0] for o in out)
        return start, jnp.max(functools.reduce(jnp.maximum, sums)), sums, tuple(o[1] for o in out)

    init = ((qi + 1) * blk, jnp.float32(0.0), (jnp.zeros((blk, LANES), F32),) * heads,
            (jnp.zeros((blk, dh), F32),) * heads)
    accs = lax.while_loop(cond, body, init)[3]
    for h in range(heads):
        o_ref[:, h * dh:(h + 1) * dh] = accs[h].astype(o_ref.dtype)


def _sb_attention(qkv, batch, seq, blk=CHUNK, win=3 * CHUNK, heads_per_step=4):
    n, three_d = qkv.shape
    d = three_d // 3
    dh = d // SB_HEADS
    groups = SB_HEADS // heads_per_step
    width = heads_per_step * dh
    nq = seq // blk
    win = min(win, seq)
    return pl.pallas_call(
        functools.partial(_sb_kernel, blk=blk, win=win, dh=dh, scale=dh ** -0.5),
        grid=(batch, groups, nq),
        in_specs=[pl.BlockSpec((blk, width), lambda b, g, i: (b * nq + i, g)),
                  pl.BlockSpec((seq, width), lambda b, g, i: (b, groups + g)),
                  pl.BlockSpec((seq, width), lambda b, g, i: (b, 2 * groups + g))],
        out_specs=pl.BlockSpec((blk, width), lambda b, g, i: (b * nq + i, g)),
        out_shape=jax.ShapeDtypeStruct((n, d), BF16),
        compiler_params=_params("arbitrary", "arbitrary", "arbitrary"),
        name="sb_attention",
    )(qkv, qkv, qkv)


def _mlstm_kernel(q_ref, k_ref, v_ref, o_ref, g_ref, bias_ref, ng_ref, y_ref, c_ref, m_ref,
                  *, dk, dv, scale):
    L = q_ref.shape[0]
    heads = ML_HEADS

    @pl.when(pl.program_id(1) == 0)
    def _():
        c_ref[...] = jnp.zeros_like(c_ref)
        m_ref[...] = jnp.zeros_like(m_ref)

    g = g_ref[...] + bias_ref[...]
    log_f = _log_sigmoid(g)
    row = lax.broadcasted_iota(jnp.int32, (L, L), 0)
    col = lax.broadcasted_iota(jnp.int32, (L, L), 1)
    causal = col <= row
    b_cols = jnp.dot(causal.astype(F32), log_f, precision=lax.Precision.HIGHEST,
                     preferred_element_type=F32)
    b_rows = b_cols.T
    g_rows = g.T
    ones_col = (lax.broadcasted_iota(jnp.int32, (L, LANES), 1) == 0).astype(BF16)

    for h in range(heads):
        q = q_ref[:, h * dk:(h + 1) * dk]
        k = k_ref[:, h * dk:(h + 1) * dk]
        v_ext = jnp.concatenate([v_ref[:, h * dv:(h + 1) * dv], ones_col], axis=1)
        li_col = g[:, h:h + 1]
        li_row = g_rows[h:h + 1, :]
        b_col = b_cols[:, heads + h:heads + h + 1]
        b_row = b_rows[heads + h:heads + h + 1, :]
        m_prev = m_ref[h:h + 1, 0:1]
        c_ext = c_ref[h]

        d_intra = jnp.where(causal, b_col - b_row + li_row, -jnp.inf)
        d_inter = b_col + m_prev
        m_t = jnp.maximum(d_inter, jnp.max(d_intra, axis=-1, keepdims=True))
        w_intra = jnp.exp(d_intra - m_t)
        w_inter = jnp.exp(d_inter - m_t)
        s = _dot_nt(q, k) * scale * w_intra
        num = _dot(s.astype(BF16), v_ext) + w_inter * (_dot(q, c_ext.astype(BF16)) * scale)
        den = num[:, dv:dv + 1]
        hid = num[:, :dv] / jnp.maximum(jnp.abs(den), jnp.exp(-m_t))

        b_last = b_col[L - 1:L, :]
        m_new = jnp.maximum(b_last + m_prev,
                            jnp.max(b_last - b_row + li_row, axis=-1, keepdims=True))
        decay = jnp.exp(b_last + m_prev - m_new)
        w_state = jnp.exp(b_last - b_col + li_col - m_new)
        kw = (k.astype(F32) * w_state).astype(BF16)
        c_ref[h] = decay * c_ext + _dot_tn(kw, v_ext)
        m_ref[h:h + 1, :] = jnp.broadcast_to(m_new, (1, LANES))

        hid = hid * lax.rsqrt(jnp.mean(hid * hid, axis=-1, keepdims=True) + LN_EPS)
        hid = hid * ng_ref[:, h * dv:(h + 1) * dv]
        gate = jax.nn.sigmoid(o_ref[:, h * dv:(h + 1) * dv])
        y_ref[:, h * dv:(h + 1) * dv] = (gate * hid).astype(y_ref.dtype)


def _mlstm(qkv, og, bias, norm_g, batch, seq):
    n = qkv.shape[0]
    heads = ML_HEADS
    d = og.shape[1] - LANES
    dv = d // heads
    dk = dv // 2
    L = CHUNK
    nc = seq // L
    tok = lambda b, c: (b * nc + c, 0)
    fixed = lambda b, c: (0, 0)
    return pl.pallas_call(
        functools.partial(_mlstm_kernel, dk=dk, dv=dv, scale=dk ** -0.5),
        grid=(batch, nc),
        in_specs=[pl.BlockSpec((L, heads * dk), tok),
                  pl.BlockSpec((L, heads * dk), lambda b, c: (b * nc + c, 1)),
                  pl.BlockSpec((L, d), lambda b, c: (b * nc + c, 1)),
                  pl.BlockSpec((L, d), tok),
                  pl.BlockSpec((L, LANES), lambda b, c: (b * nc + c, d // LANES)),
                  pl.BlockSpec((1, LANES), fixed),
                  pl.BlockSpec((1, d), fixed)],
        out_specs=pl.BlockSpec((L, d), tok),
        out_shape=jax.ShapeDtypeStruct((n, d), BF16),
        scratch_shapes=[pltpu.VMEM((heads, dk, dv + LANES), F32),
                        pltpu.VMEM((heads, LANES), F32)],
        compiler_params=_params("arbitrary", "arbitrary"),
        name="mlstm",
    )(qkv, qkv, qkv, og, og, bias, norm_g.reshape(1, d))


def _sgu_kernel(u_ref, v_ref, g_ref, b_ref, ws_ref, bs_ref, y_ref):
    L, width = u_ref.shape
    dg = width // SG_GROUPS
    vn = _layer_norm(v_ref[...], g_ref[...], b_ref[...]).astype(BF16)
    row = lax.broadcasted_iota(jnp.int32, (L, L), 0)
    col = lax.broadcasted_iota(jnp.int32, (L, L), 1)
    for g in range(SG_GROUPS):
        ws = jnp.where(col <= row, ws_ref[g], 0.0).astype(BF16)
        mixed = _dot(ws, vn[:, g * dg:(g + 1) * dg]) + bs_ref[:, g:g + 1]
        y_ref[:, g * dg:(g + 1) * dg] = (u_ref[:, g * dg:(g + 1) * dg] * mixed).astype(y_ref.dtype)


def _sgu(uv, norm_g, norm_b, w_s, b_s):
    n, two_w = uv.shape
    width = two_w // 2
    L = CHUNK
    bs_t = jnp.zeros((L, LANES), F32).at[:, :SG_GROUPS].set(b_s.T)
    fixed = lambda i: (0, 0)
    return pl.pallas_call(
        _sgu_kernel,
        grid=(n // L,),
        in_specs=[pl.BlockSpec((L, width), lambda i: (i, 0)),
                  pl.BlockSpec((L, width), lambda i: (i, 1)),
                  pl.BlockSpec((1, width), fixed), pl.BlockSpec((1, width), fixed),
                  pl.BlockSpec((SG_GROUPS, L, L), lambda i: (0, 0, 0)),
                  pl.BlockSpec((L, LANES), fixed)],
        out_specs=pl.BlockSpec((L, width), lambda i: (i, 0)),
        out_shape=jax.ShapeDtypeStruct((n, width), BF16),
        compiler_params=_params("arbitrary"),
        name="sgu",
    )(uv, uv, norm_g.reshape(1, width), norm_b.reshape(1, width), w_s, bs_t)


def _extract_top(x, count, first_only):
    rows = x.shape[0]
    iota = lax.broadcasted_iota(jnp.int32, x.shape, 0)
    rank = jnp.full(x.shape, NOT_RANKED, F32)
    vals = []
    for k in range(count):
        mx = jnp.max(x, axis=0, keepdims=True)
        sel = x == mx
        if first_only:
            sel = iota == jnp.min(jnp.where(sel, iota, rows), axis=0, keepdims=True)
        vals.append(mx)
        rank = jnp.where(sel, float(k + 1), rank)
        x = jnp.where(sel, -jnp.inf, x)
    return vals, rank


def _count_ranked(rank):
    return jnp.sum((rank < NOT_RANKED).astype(F32), axis=0, keepdims=True)


def _route_one_head(s1, s2, first_only):
    K = PEER_TOPK
    T = s1.shape[1]
    a, rank1 = _extract_top(s1, K, first_only)
    b, rank2 = _extract_top(s2, K, first_only)

    b_all = jnp.concatenate(b, axis=0)
    sub = lax.broadcasted_iota(jnp.int32, (8, T), 0)
    blocks = [a[0] + b_all]
    for p in range(1, K):
        blocks.append(jnp.where(sub < K // (p + 1), a[p] + b_all[:8], -jnp.inf))
    cand = jnp.concatenate(blocks, axis=0)
    best, order = _extract_top(cand, K, first_only)
    chosen = (order < NOT_RANKED).astype(F32)
    counts = [jnp.sum(chosen[:K], axis=0, keepdims=True)]
    for p in range(1, K):
        counts.append(jnp.sum(chosen[K + 8 * (p - 1):K + 8 * p], axis=0, keepdims=True))
    ranked = jnp.maximum(jnp.maximum(_count_ranked(rank1), _count_ranked(rank2)),
                         functools.reduce(jnp.add, counts))

    z = jnp.zeros_like(best[0])
    for k in range(K):
        z = z + jnp.exp(best[k] - best[0])
    c1 = jnp.zeros_like(s1)
    for p in range(K):
        c1 = jnp.where(rank1 == float(p + 1), counts[p], c1)
    return rank2, c1, jnp.exp(s1 - a[0]) / z, jnp.exp(s2 - b[0]), ranked


def _peer_route_kernel(q_ref, keys_ref, r2_ref, c1_ref, e1_ref, e2_ref):
    half = keys_ref.shape[3]

    def route(first_only):
        ranked = []
        for h in range(keys_ref.shape[0]):
            s1 = _dot_nt(keys_ref[h, 0], q_ref[:, 2 * h * half:(2 * h + 1) * half])
            s2 = _dot_nt(keys_ref[h, 1], q_ref[:, (2 * h + 1) * half:(2 * h + 2) * half])
            r2, c1_ref[h], e1_ref[h], e2, n = _route_one_head(s1, s2, first_only)
            r2_ref[h] = r2.astype(r2_ref.dtype)
            e2_ref[h] = e2.astype(e2_ref.dtype)
            ranked.append(n)
        return jnp.max(functools.reduce(jnp.maximum, ranked))

    most_ranked = route(False)

    @pl.when(most_ranked > PEER_TOPK)
    def _():
        route(True)


def _peer_route(q, keys, tt=LANES, heads_per_step=2):
    n = q.shape[0]
    heads, _, nk, half = keys.shape
    hp = heads_per_step
    out = lambda dtype: jax.ShapeDtypeStruct((heads, nk, n), dtype)
    spec = pl.BlockSpec((hp, nk, tt), lambda i, g: (g, 0, i))
    return pl.pallas_call(
        _peer_route_kernel,
        grid=(n // tt, heads // hp),
        in_specs=[pl.BlockSpec((tt, 2 * half * hp), lambda i, g: (i, g)),
                  pl.BlockSpec((hp, 2, nk, half), lambda i, g: (g, 0, 0, 0))],
        out_specs=[spec] * 4,
        out_shape=[out(BF16), out(F32), out(F32), out(BF16)],
        compiler_params=_params("arbitrary", "arbitrary"),
        name="peer_route",
    )(q, keys)


def _peer_expert_kernel(x_ref, u_ref, vta_ref, vtb_ref, r2_ref, e2_ref, c1_ref, e1_ref, o_ref,
                        pa_ref, pb_ref, *, lane_chunk, sub_experts):
    e = pl.program_id(1)
    last = pl.num_programs(1) - 1
    nk = PEER_KEYS
    tb = x_ref.shape[0]
    half = u_ref.shape[0] // 2
    n_sub = half // sub_experts
    d_rows = vta_ref.shape[0] // n_sub

    def rows_to_tile(ref, h, ii, ts):
        row = jnp.broadcast_to(ref[h, ii:ii + 1, ts], (BF16_SUBLANES, lane_chunk)).astype(BF16)
        return jnp.concatenate([row] * (nk // BF16_SUBLANES), axis=0)

    def evaluate(p_ref, first, c):
        rows = slice(first + c * sub_experts, first + (c + 1) * sub_experts)
        act = _gelu(_dot_nt(u_ref[rows, :], x_ref[...])).astype(BF16)
        for r0 in range(0, sub_experts, nk):
            ii = (rows.start + r0) // nk
            for t0 in range(0, tb, lane_chunk):
                ts = slice(t0, t0 + lane_chunk)
                gate = jnp.zeros((nk, lane_chunk), BF16)
                for h in range(PEER_HEADS):
                    picked = r2_ref[h, :, ts] <= rows_to_tile(c1_ref, h, ii, ts)
                    gate = gate + jnp.where(picked, e2_ref[h, :, ts], 0.0) * rows_to_tile(e1_ref, h, ii, ts)
                p_ref[c * sub_experts + r0:c * sub_experts + r0 + nk, ts] = gate * act[r0:r0 + nk, ts]

    def apply(vt_ref, p_ref, c):
        rows = slice(c * d_rows, (c + 1) * d_rows)
        o_ref[rows, :] += _dot(vt_ref[rows, :], p_ref[...])

    @pl.when(e == 0)
    def _():
        o_ref[...] = jnp.zeros_like(o_ref)
        pb_ref[...] = jnp.zeros_like(pb_ref)

    @pl.when(e < last)
    def _():
        for c in range(n_sub):
            apply(vta_ref, pb_ref, c)
            evaluate(pa_ref, 0, c)
        for c in range(n_sub):
            apply(vtb_ref, pa_ref, c)
            evaluate(pb_ref, half, c)

    @pl.when(e == last)
    def _():
        for c in range(n_sub):
            apply(vta_ref, pb_ref, c)


def _peer_experts(x, u, vt, r2, e2, c1, e1, tb=512, eb=1024, lane_chunk=256, sub_experts=256):
    n, d = x.shape
    ne = u.shape[0] // eb
    heads, nk, _ = r2.shape
    tb = min(tb, n)
    rows_i = eb // nk
    half = eb // 2
    tok = lambda t, e: (0, 0, t)
    cur = lambda e: jnp.minimum(e, ne - 1)
    return pl.pallas_call(
        functools.partial(_peer_expert_kernel, lane_chunk=min(lane_chunk, tb), sub_experts=sub_experts),
        grid=(n // tb, ne + 1),
        in_specs=[pl.BlockSpec((tb, d), lambda t, e: (t, 0)),
                  pl.BlockSpec((eb, d), lambda t, e: (cur(e), 0)),
                  pl.BlockSpec((d, half), lambda t, e: (0, jnp.maximum(2 * e - 1, 0))),
                  pl.BlockSpec((d, half), lambda t, e: (0, 2 * cur(e))),
                  pl.BlockSpec((heads, nk, tb), tok),
                  pl.BlockSpec((heads, nk, tb), tok),
                  pl.BlockSpec((heads, rows_i, tb), lambda t, e: (0, cur(e), t)),
                  pl.BlockSpec((heads, rows_i, tb), lambda t, e: (0, cur(e), t))],
        out_specs=pl.BlockSpec((d, tb), lambda t, e: (0, t)),
        out_shape=jax.ShapeDtypeStruct((d, n), F32),
        scratch_shapes=[pltpu.VMEM((half, tb), BF16), pltpu.VMEM((half, tb), BF16)],
        compiler_params=_params("arbitrary", "arbitrary"),
        name="peer_experts",
    )(x, u, vt, vt, r2, e2, c1, e1)


def _residual_ln_t_kernel(h_ref, yt_ref, g_ref, b_ref, o_ref, ob_ref):
    out = _layer_norm(ALPHA * h_ref[...] + yt_ref[...].T, g_ref[...], b_ref[...])
    o_ref[...] = out
    ob_ref[...] = out.astype(BF16)


def _residual_ln_t(h, yt, g, b, tm=512):
    n, d = h.shape
    tm = min(tm, n)
    row = lambda i: (i, 0)
    fixed = lambda i: (0, 0)
    return pl.pallas_call(
        _residual_ln_t_kernel,
        grid=(n // tm,),
        in_specs=[pl.BlockSpec((tm, d), row), pl.BlockSpec((d, tm), lambda i: (0, i)),
                  pl.BlockSpec((1, d), fixed), pl.BlockSpec((1, d), fixed)],
        out_specs=[pl.BlockSpec((tm, d), row), pl.BlockSpec((tm, d), row)],
        out_shape=[jax.ShapeDtypeStruct((n, d), F32), jax.ShapeDtypeStruct((n, d), BF16)],
        compiler_params=_params("arbitrary"),
        name="residual_ln_t",
    )(h, yt, g.reshape(1, d), b.reshape(1, d))


def _peer_layer(h, hb, w_q, sub_keys, u_tab, v_tab, ln_g, ln_b):
    q = _matmul(hb, w_q.astype(BF16), BF16, name="peer_query")
    r2, c1, e1, e2 = _peer_route(q, sub_keys.astype(BF16))
    yt = _peer_experts(hb, u_tab.astype(BF16), v_tab.T.astype(BF16), r2, e2, c1, e1)
    return _residual_ln_t(h, yt, ln_g, ln_b)


def _sb_mixer(hb, batch, seq, w_in, w_out):
    qkv = _matmul(hb, w_in.astype(BF16), BF16, name="sb_in")
    return _sb_attention(qkv, batch, seq), w_out


def _ml_mixer(hb, batch, seq, w_in, b_gates, norm_g, w_out):
    heads = ML_HEADS
    d_gate = w_out.shape[0]
    n_qkv = w_in.shape[1] - d_gate - 2 * heads
    qkv = _matmul(hb, w_in[:, :n_qkv].astype(BF16), BF16, name="ml_in_qkv")
    w_og = jnp.pad(w_in[:, n_qkv:], ((0, 0), (0, LANES - 2 * heads))).astype(BF16)
    og = _matmul(hb, w_og, F32, tm=512, tn=w_og.shape[1], name="ml_in_gates")
    bias = jnp.pad(b_gates.astype(F32), (0, LANES - 2 * heads)).reshape(1, LANES)
    return _mlstm(qkv, og, bias, norm_g, batch, seq), w_out


def _sg_mixer(hb, batch, seq, w_in, norm_g, norm_b, w_s, b_s, w_out):
    uv = _matmul(hb, w_in.astype(BF16), F32, act="gelu", name="sg_in")
    return _sgu(uv, norm_g, norm_b, w_s, b_s), w_out


def kernel(x, l0_sb_w_in, l0_sb_w_out, l0_ln1_g, l0_ln1_b, l0_peer_w_q, l0_peer_sub_keys, l0_peer_u, l0_peer_v, l0_ln2_g, l0_ln2_b, l1_ml_w_in, l1_ml_b_gates, l1_ml_norm_g, l1_ml_w_out, l1_ln1_g, l1_ln1_b, l1_peer_w_q, l1_peer_sub_keys, l1_peer_u, l1_peer_v, l1_ln2_g, l1_ln2_b, l2_sg_w_in, l2_sg_norm_g, l2_sg_norm_b, l2_sg_w_s, l2_sg_b_s, l2_sg_w_out, l2_ln1_g, l2_ln1_b, l2_peer_w_q, l2_peer_sub_keys, l2_peer_u, l2_peer_v, l2_ln2_g, l2_ln2_b, l3_sb_w_in, l3_sb_w_out, l3_ln1_g, l3_ln1_b, l3_peer_w_q, l3_peer_sub_keys, l3_peer_u, l3_peer_v, l3_ln2_g, l3_ln2_b):
    batch, seq, d = x.shape
    mixers = (
        (_sb_mixer, (l0_sb_w_in, l0_sb_w_out)),
        (_ml_mixer, (l1_ml_w_in, l1_ml_b_gates, l1_ml_norm_g, l1_ml_w_out)),
        (_sg_mixer, (l2_sg_w_in, l2_sg_norm_g, l2_sg_norm_b, l2_sg_w_s, l2_sg_b_s, l2_sg_w_out)),
        (_sb_mixer, (l3_sb_w_in, l3_sb_w_out)),
    )
    norm1 = ((l0_ln1_g, l0_ln1_b), (l1_ln1_g, l1_ln1_b), (l2_ln1_g, l2_ln1_b), (l3_ln1_g, l3_ln1_b))
    peers = (
        (l0_peer_w_q, l0_peer_sub_keys, l0_peer_u, l0_peer_v, l0_ln2_g, l0_ln2_b),
        (l1_peer_w_q, l1_peer_sub_keys, l1_peer_u, l1_peer_v, l1_ln2_g, l1_ln2_b),
        (l2_peer_w_q, l2_peer_sub_keys, l2_peer_u, l2_peer_v, l2_ln2_g, l2_ln2_b),
        (l3_peer_w_q, l3_peer_sub_keys, l3_peer_u, l3_peer_v, l3_ln2_g, l3_ln2_b),
    )
    h = x.reshape(batch * seq, d)
    hb = h.astype(BF16)
    for (mixer, mixer_params), (g1, b1), peer_params in zip(mixers, norm1, peers):
        pre, w_out = mixer(hb, batch, seq, *mixer_params)
        h, hb = _proj_residual_ln(pre, w_out.astype(BF16), h, g1, b1)
        h, hb = _peer_layer(h, hb, *peer_params)
    return h.reshape(batch, seq, d)
```

```python
import functools

import jax
import jax.numpy as jnp
from jax import lax
from jax.experimental import pallas as pl
from jax.experimental.pallas import tpu as pltpu

F32 = jnp.float32
BF16 = jnp.bfloat16

LANES = 128
BF16_SUBLANES = 16
CHUNK = 128
SB_HEADS = 16
ML_HEADS = 8
SG_GROUPS = 8
PEER_HEADS = 8
PEER_KEYS = 128
PEER_TOPK = 16
DEPTH = 4
ALPHA = (2 * DEPTH) ** 0.25
LN_EPS = 1e-5
VMEM_LIMIT_BYTES = 56 * 1024 * 1024
EXP_ZERO_BELOW = -110.0
NOT_RANKED = 99.0


def _params(*sem):
    return pltpu.CompilerParams(dimension_semantics=sem, vmem_limit_bytes=VMEM_LIMIT_BYTES)


def _gelu(x):
    return 0.5 * x * (1.0 + lax.erf(x * 0.7071067811865476))


def _softplus(z):
    return jnp.maximum(z, 0.0) + jnp.log1p(jnp.exp(-jnp.abs(z)))


def _log_sigmoid(z):
    return -_softplus(-z)


def _layer_norm(t, g, b):
    mu = jnp.mean(t, axis=-1, keepdims=True)
    d = t - mu
    var = jnp.mean(d * d, axis=-1, keepdims=True)
    return d * lax.rsqrt(var + LN_EPS) * g + b


def _dot(a, b):
    return jnp.dot(a, b, preferred_element_type=F32)


def _dot_nt(a, b):
    return lax.dot_general(a, b, (((1,), (1,)), ((), ())), preferred_element_type=F32)


def _dot_tn(a, b):
    return lax.dot_general(a, b, (((0,), (0,)), ((), ())), preferred_element_type=F32)


def _mm_kernel(x_ref, w_ref, o_ref, *, act):
    y = _dot(x_ref[...], w_ref[...])
    if act == "gelu":
        y = _gelu(y)
    o_ref[...] = y.astype(o_ref.dtype)


def _matmul(x, w, out_dtype, act=None, tm=1024, tn=1024, name="matmul"):
    m, k = x.shape
    n = w.shape[1]
    tm, tn = min(tm, m), min(tn, n)
    assert m % tm == 0 and n % tn == 0, (m, n, tm, tn)
    return pl.pallas_call(
        functools.partial(_mm_kernel, act=act),
        grid=(n // tn, m // tm),
        in_specs=[pl.BlockSpec((tm, k), lambda j, i: (i, 0)),
                  pl.BlockSpec((k, tn), lambda j, i: (0, j))],
        out_specs=pl.BlockSpec((tm, tn), lambda j, i: (i, j)),
        out_shape=jax.ShapeDtypeStruct((m, n), out_dtype),
        compiler_params=_params("arbitrary", "arbitrary"),
        name=name,
    )(x, w)


def _proj_ln_kernel(x_ref, w_ref, h_ref, g_ref, b_ref, o_ref, ob_ref):
    y = _dot(x_ref[...], w_ref[...])
    out = _layer_norm(ALPHA * h_ref[...] + y, g_ref[...], b_ref[...])
    o_ref[...] = out
    ob_ref[...] = out.astype(BF16)


def _proj_residual_ln(x, w, h, g, b, tm=512):
    m, k = x.shape
    d = w.shape[1]
    tm = min(tm, m)
    assert m % tm == 0
    row = lambda i: (i, 0)
    fixed = lambda i: (0, 0)
    return pl.pallas_call(
        _proj_ln_kernel,
        grid=(m // tm,),
        in_specs=[pl.BlockSpec((tm, k), row), pl.BlockSpec((k, d), fixed),
                  pl.BlockSpec((tm, d), row), pl.BlockSpec((1, d), fixed),
                  pl.BlockSpec((1, d), fixed)],
        out_specs=[pl.BlockSpec((tm, d), row), pl.BlockSpec((tm, d), row)],
        out_shape=[jax.ShapeDtypeStruct((m, d), F32), jax.ShapeDtypeStruct((m, d), BF16)],
        compiler_params=_params("arbitrary"),
        name="proj_residual_ln",
    )(x, w, h, g.reshape(1, d), b.reshape(1, d))


def _sb_kernel(q_ref, k_ref, v_ref, o_ref, *, blk, win, dh, scale):
    qi = pl.program_id(2)
    heads = q_ref.shape[1] // dh
    row = lax.broadcasted_iota(jnp.int32, (win, win), 0)
    col = lax.broadcasted_iota(jnp.int32, (win, win), 1)
    suffix = jnp.concatenate(
        [(row > col).astype(BF16), jnp.ones((win, LANES), BF16)], axis=1)
    q_pos = qi * blk + lax.broadcasted_iota(jnp.int32, (blk, win), 0)
    offset = lax.broadcasted_iota(jnp.int32, (blk, win), 1)

    def window(start, valid, h, carry_sum, acc):
        cols = slice(h * dh, (h + 1) * dh)
        k = k_ref[pl.ds(start, win), cols]
        v = v_ref[pl.ds(start, win), cols]
        z = _dot_nt(q_ref[:, cols], k) * scale
        sp = _softplus(z)
        log_keep = jnp.where(valid, -sp, 0.0)
        hi = log_keep.astype(BF16)
        lo = (log_keep - hi.astype(F32)).astype(BF16)
        sums = _dot(hi, suffix) + _dot(lo, suffix)
        between = sums[:, :win] + jnp.concatenate([carry_sum] * (win // LANES), axis=1)
        w = jnp.where(valid, jnp.exp(z - sp + between), 0.0)
        return carry_sum + sums[:, win:], acc + _dot(w.astype(BF16), v)

    def cond(c):
        return jnp.logical_and(c[0] > 0, c[1] > EXP_ZERO_BELOW)

    def body(c):
        end, _, sums, accs = c
        start = pl.multiple_of(jnp.maximum(end - win, 0), blk)
        key_pos = start + offset
        valid = jnp.logical_and(key_pos < q_pos, key_pos < end)
        out = [window(start, valid, h, sums[h], accs[h]) for h in range(heads)]
        sums = tuple(o[0] for o in out)
        return start, jnp.max(functools.reduce(jnp.maximum, sums)), sums, tuple(o[1] for o in out)

    init = ((qi + 1) * blk, jnp.float32(0.0), (jnp.zeros((blk, LANES), F32),) * heads,
            (jnp.zeros((blk, dh), F32),) * heads)
    accs = lax.while_loop(cond, body, init)[3]
    for h in range(heads):
        o_ref[:, h * dh:(h + 1) * dh] = accs[h].astype(o_ref.dtype)


def _sb_attention(qkv, batch, seq, blk=CHUNK, win=3 * CHUNK, heads_per_step=8):
    n, three_d = qkv.shape
    d = three_d // 3
    dh = d // SB_HEADS
    groups = SB_HEADS // heads_per_step
    width = heads_per_step * dh
    nq = seq // blk
    win = min(win, seq)
    return pl.pallas_call(
        functools.partial(_sb_kernel, blk=blk, win=win, dh=dh, scale=dh ** -0.5),
        grid=(batch, groups, nq),
        in_specs=[pl.BlockSpec((blk, width), lambda b, g, i: (b * nq + i, g)),
                  pl.BlockSpec((seq, width), lambda b, g, i: (b, groups + g)),
                  pl.BlockSpec((seq, width), lambda b, g, i: (b, 2 * groups + g))],
        out_specs=pl.BlockSpec((blk, width), lambda b, g, i: (b * nq + i, g)),
        out_shape=jax.ShapeDtypeStruct((n, d), BF16),
        compiler_params=_params("arbitrary", "arbitrary", "arbitrary"),
        name="sb_attention",
    )(qkv, qkv, qkv)


def _mlstm_kernel(q_ref, k_ref, v_ref, o_ref, g_ref, bias_ref, ng_ref, y_ref, c_ref, m_ref,
                  *, dk, dv, scale):
    L = q_ref.shape[0]
    heads = ML_HEADS

    @pl.when(pl.program_id(1) == 0)
    def _():
        c_ref[...] = jnp.zeros_like(c_ref)
        m_ref[...] = jnp.zeros_like(m_ref)

    g = g_ref[...] + bias_ref[...]
    log_f = _log_sigmoid(g)
    row = lax.broadcasted_iota(jnp.int32, (L, L), 0)
    col = lax.broadcasted_iota(jnp.int32, (L, L), 1)
    causal = col <= row
    b_cols = jnp.dot(causal.astype(F32), log_f, precision=lax.Precision.HIGHEST,
                     preferred_element_type=F32)
    b_rows = b_cols.T
    g_rows = g.T
    ones_col = (lax.broadcasted_iota(jnp.int32, (L, LANES), 1) == 0).astype(BF16)

    for h in range(heads):
        q = q_ref[:, h * dk:(h + 1) * dk]
        k = k_ref[:, h * dk:(h + 1) * dk]
        v_ext = jnp.concatenate([v_ref[:, h * dv:(h + 1) * dv], ones_col], axis=1)
        li_col = g[:, h:h + 1]
        li_row = g_rows[h:h + 1, :]
        b_col = b_cols[:, heads + h:heads + h + 1]
        b_row = b_rows[heads + h:heads + h + 1, :]
        m_prev = m_ref[h:h + 1, 0:1]
        c_ext = c_ref[h]

        d_intra = jnp.where(causal, b_col - b_row + li_row, -jnp.inf)
        d_inter = b_col + m_prev
        m_t = jnp.maximum(d_inter, jnp.max(d_intra, axis=-1, keepdims=True))
        w_intra = jnp.exp(d_intra - m_t)
        w_inter = jnp.exp(d_inter - m_t)
        s = _dot_nt(q, k) * scale * w_intra
        num = _dot(s.astype(BF16), v_ext) + w_inter * (_dot(q, c_ext.astype(BF16)) * scale)
        den = num[:, dv:dv + 1]
        hid = num[:, :dv] / jnp.maximum(jnp.abs(den), jnp.exp(-m_t))

        b_last = b_col[L - 1:L, :]
        m_new = jnp.maximum(b_last + m_prev,
                            jnp.max(b_last - b_row + li_row, axis=-1, keepdims=True))
        decay = jnp.exp(b_last + m_prev - m_new)
        w_state = jnp.exp(b_last - b_col + li_col - m_new)
        kw = (k.astype(F32) * w_state).astype(BF16)
        c_ref[h] = decay * c_ext + _dot_tn(kw, v_ext)
        m_ref[h:h + 1, :] = jnp.broadcast_to(m_new, (1, LANES))

        hid = hid * lax.rsqrt(jnp.mean(hid * hid, axis=-1, keepdims=True) + LN_EPS)
        hid = hid * ng_ref[:, h * dv:(h + 1) * dv]
        gate = jax.nn.sigmoid(o_ref[:, h * dv:(h + 1) * dv])
        y_ref[:, h * dv:(h + 1) * dv] = (gate * hid).astype(y_ref.dtype)


def _mlstm(qkv, og, bias, norm_g, batch, seq):
    n = qkv.shape[0]
    heads = ML_HEADS
    d = og.shape[1] - LANES
    dv = d // heads
    dk = dv // 2
    L = CHUNK
    nc = seq // L
    tok = lambda b, c: (b * nc + c, 0)
    fixed = lambda b, c: (0, 0)
    return pl.pallas_call(
        functools.partial(_mlstm_kernel, dk=dk, dv=dv, scale=dk ** -0.5),
        grid=(batch, nc),
        in_specs=[pl.BlockSpec((L, heads * dk), tok),
                  pl.BlockSpec((L, heads * dk), lambda b, c: (b * nc + c, 1)),
                  pl.BlockSpec((L, d), lambda b, c: (b * nc + c, 1)),
                  pl.BlockSpec((L, d), tok),
                  pl.BlockSpec((L, LANES), lambda b, c: (b * nc + c, d // LANES)),
                  pl.BlockSpec((1, LANES), fixed),
                  pl.BlockSpec((1, d), fixed)],
        out_specs=pl.BlockSpec((L, d), tok),
        out_shape=jax.ShapeDtypeStruct((n, d), BF16),
        scratch_shapes=[pltpu.VMEM((heads, dk, dv + LANES), F32),
                        pltpu.VMEM((heads, LANES), F32)],
        compiler_params=_params("arbitrary", "arbitrary"),
        name="mlstm",
    )(qkv, qkv, qkv, og, og, bias, norm_g.reshape(1, d))


def _sgu_kernel(u_ref, v_ref, g_ref, b_ref, ws_ref, bs_ref, y_ref):
    L, width = u_ref.shape
    dg = width // SG_GROUPS
    vn = _layer_norm(v_ref[...], g_ref[...], b_ref[...]).astype(BF16)
    row = lax.broadcasted_iota(jnp.int32, (L, L), 0)
    col = lax.broadcasted_iota(jnp.int32, (L, L), 1)
    for g in range(SG_GROUPS):
        ws = jnp.where(col <= row, ws_ref[g], 0.0).astype(BF16)
        mixed = _dot(ws, vn[:, g * dg:(g + 1) * dg]) + bs_ref[:, g:g + 1]
        y_ref[:, g * dg:(g + 1) * dg] = (u_ref[:, g * dg:(g + 1) * dg] * mixed).astype(y_ref.dtype)


def _sgu(uv, norm_g, norm_b, w_s, b_s):
    n, two_w = uv.shape
    width = two_w // 2
    L = CHUNK
    bs_t = jnp.zeros((L, LANES), F32).at[:, :SG_GROUPS].set(b_s.T)
    fixed = lambda i: (0, 0)
    return pl.pallas_call(
        _sgu_kernel,
        grid=(n // L,),
        in_specs=[pl.BlockSpec((L, width), lambda i: (i, 0)),
                  pl.BlockSpec((L, width), lambda i: (i, 1)),
                  pl.BlockSpec((1, width), fixed), pl.BlockSpec((1, width), fixed),
                  pl.BlockSpec((SG_GROUPS, L, L), lambda i: (0, 0, 0)),
                  pl.BlockSpec((L, LANES), fixed)],
        out_specs=pl.BlockSpec((L, width), lambda i: (i, 0)),
        out_shape=jax.ShapeDtypeStruct((n, width), BF16),
        compiler_params=_params("arbitrary"),
        name="sgu",
    )(uv, uv, norm_g.reshape(1, width), norm_b.reshape(1, width), w_s, bs_t)


def _extract_top(x, count, first_only):
    rows = x.shape[0]
    iota = lax.broadcasted_iota(jnp.int32, x.shape, 0)
    rank = jnp.full(x.shape, NOT_RANKED, F32)
    vals = []
    for k in range(count):
        mx = jnp.max(x, axis=0, keepdims=True)
        sel = x == mx
        if first_only:
            sel = iota == jnp.min(jnp.where(sel, iota, rows), axis=0, keepdims=True)
        vals.append(mx)
        rank = jnp.where(sel, float(k + 1), rank)
        x = jnp.where(sel, -jnp.inf, x)
    return vals, rank


def _count_ranked(rank):
    return jnp.sum((rank < NOT_RANKED).astype(F32), axis=0, keepdims=True)


def _route_one_head(s1, s2, first_only):
    K = PEER_TOPK
    T = s1.shape[1]
    a, rank1 = _extract_top(s1, K, first_only)
    b, rank2 = _extract_top(s2, K, first_only)

    b_all = jnp.concatenate(b, axis=0)
    sub = lax.broadcasted_iota(jnp.int32, (8, T), 0)
    blocks = [a[0] + b_all]
    for p in range(1, K):
        blocks.append(jnp.where(sub < K // (p + 1), a[p] + b_all[:8], -jnp.inf))
    cand = jnp.concatenate(blocks, axis=0)
    best, order = _extract_top(cand, K, first_only)
    chosen = (order < NOT_RANKED).astype(F32)
    counts = [jnp.sum(chosen[:K], axis=0, keepdims=True)]
    for p in range(1, K):
        counts.append(jnp.sum(chosen[K + 8 * (p - 1):K + 8 * p], axis=0, keepdims=True))
    ranked = jnp.maximum(jnp.maximum(_count_ranked(rank1), _count_ranked(rank2)),
                         functools.reduce(jnp.add, counts))

    z = jnp.zeros_like(best[0])
    for k in range(K):
        z = z + jnp.exp(best[k] - best[0])
    c1 = jnp.zeros_like(s1)
    for p in range(K):
        c1 = jnp.where(rank1 == float(p + 1), counts[p], c1)
    return rank2, c1, jnp.exp(s1 - a[0]) / z, jnp.exp(s2 - b[0]), ranked


def _peer_route_kernel(q_ref, keys_ref, r2_ref, c1_ref, e1_ref, e2_ref):
    half = keys_ref.shape[3]

    def route(first_only):
        ranked = []
        for h in range(keys_ref.shape[0]):
            s1 = _dot_nt(keys_ref[h, 0], q_ref[:, 2 * h * half:(2 * h + 1) * half])
            s2 = _dot_nt(keys_ref[h, 1], q_ref[:, (2 * h + 1) * half:(2 * h + 2) * half])
            r2, c1_ref[h], e1_ref[h], e2, n = _route_one_head(s1, s2, first_only)
            r2_ref[h] = r2.astype(r2_ref.dtype)
            e2_ref[h] = e2.astype(e2_ref.dtype)
            ranked.append(n)
        return jnp.max(functools.reduce(jnp.maximum, ranked))

    most_ranked = route(False)

    @pl.when(most_ranked > PEER_TOPK)
    def _():
        route(True)


def _peer_route(q, keys, tt=LANES, heads_per_step=4):
    n = q.shape[0]
    heads, _, nk, half = keys.shape
    hp = heads_per_step
    out = lambda dtype: jax.ShapeDtypeStruct((heads, nk, n), dtype)
    spec = pl.BlockSpec((hp, nk, tt), lambda i, g: (g, 0, i))
    return pl.pallas_call(
        _peer_route_kernel,
        grid=(n // tt, heads // hp),
        in_specs=[pl.BlockSpec((tt, 2 * half * hp), lambda i, g: (i, g)),
                  pl.BlockSpec((hp, 2, nk, half), lambda i, g: (g, 0, 0, 0))],
        out_specs=[spec] * 4,
        out_shape=[out(BF16), out(F32), out(F32), out(BF16)],
        compiler_params=_params("arbitrary", "arbitrary"),
        name="peer_route",
    )(q, keys)


def _peer_expert_kernel(x_ref, u_ref, vta_ref, vtb_ref, r2_ref, e2_ref, c1_ref, e1_ref, o_ref,
                        pa_ref, pb_ref, *, lane_chunk, sub_experts):
    e = pl.program_id(1)
    last = pl.num_programs(1) - 1
    nk = PEER_KEYS
    tb = x_ref.shape[0]
    half = u_ref.shape[0] // 2
    n_sub = half // sub_experts
    d_rows = vta_ref.shape[0] // n_sub

    def rows_to_tile(ref, h, ii, ts):
        row = jnp.broadcast_to(ref[h, ii:ii + 1, ts], (BF16_SUBLANES, lane_chunk)).astype(BF16)
        return jnp.concatenate([row] * (nk // BF16_SUBLANES), axis=0)

    def evaluate(p_ref, first, c):
        rows = slice(first + c * sub_experts, first + (c + 1) * sub_experts)
        act = _gelu(_dot_nt(u_ref[rows, :], x_ref[...])).astype(BF16)
        for r0 in range(0, sub_experts, nk):
            ii = (rows.start + r0) // nk
            for t0 in range(0, tb, lane_chunk):
                ts = slice(t0, t0 + lane_chunk)
                gate = jnp.zeros((nk, lane_chunk), BF16)
                for h in range(PEER_HEADS):
                    picked = r2_ref[h, :, ts] <= rows_to_tile(c1_ref, h, ii, ts)
                    gate = gate + jnp.where(picked, e2_ref[h, :, ts], 0.0) * rows_to_tile(e1_ref, h, ii, ts)
                p_ref[c * sub_experts + r0:c * sub_experts + r0 + nk, ts] = gate * act[r0:r0 + nk, ts]

    def apply(vt_ref, p_ref, c):
        rows = slice(c * d_rows, (c + 1) * d_rows)
        o_ref[rows, :] += _dot(vt_ref[rows, :], p_ref[...])

    @pl.when(e == 0)
    def _():
        o_ref[...] = jnp.zeros_like(o_ref)
        pb_ref[...] = jnp.zeros_like(pb_ref)

    @pl.when(e < last)
    def _():
        for c in range(n_sub):
            apply(vta_ref, pb_ref, c)
            evaluate(pa_ref, 0, c)
        for c in range(n_sub):
            apply(vtb_ref, pa_ref, c)
            evaluate(pb_ref, half, c)

    @pl.when(e == last)
    def _():
        for c in range(n_sub):
            apply(vta_ref, pb_ref, c)


def _peer_experts(x, u, vt, r2, e2, c1, e1, tb=512, eb=1024, lane_chunk=256, sub_experts=256):
    n, d = x.shape
    ne = u.shape[0] // eb
    heads, nk, _ = r2.shape
    tb = min(tb, n)
    rows_i = eb // nk
    half = eb // 2
    tok = lambda t, e: (0, 0, t)
    cur = lambda e: jnp.minimum(e, ne - 1)
    return pl.pallas_call(
        functools.partial(_peer_expert_kernel, lane_chunk=min(lane_chunk, tb), sub_experts=sub_experts),
        grid=(n // tb, ne + 1),
        in_specs=[pl.BlockSpec((tb, d), lambda t, e: (t, 0)),
                  pl.BlockSpec((eb, d), lambda t, e: (cur(e), 0)),
                  pl.BlockSpec((d, half), lambda t, e: (0, jnp.maximum(2 * e - 1, 0))),
                  pl.BlockSpec((d, half), lambda t, e: (0, 2 * cur(e))),
                  pl.BlockSpec((heads, nk, tb), tok),
                  pl.BlockSpec((heads, nk, tb), tok),
                  pl.BlockSpec((heads, rows_i, tb), lambda t, e: (0, cur(e), t)),
                  pl.BlockSpec((heads, rows_i, tb), lambda t, e: (0, cur(e), t))],
        out_specs=pl.BlockSpec((d, tb), lambda t, e: (0, t)),
        out_shape=jax.ShapeDtypeStruct((d, n), F32),
        scratch_shapes=[pltpu.VMEM((half, tb), BF16), pltpu.VMEM((half, tb), BF16)],
        compiler_params=_params("arbitrary", "arbitrary"),
        name="peer_experts",
    )(x, u, vt, vt, r2, e2, c1, e1)


def _residual_ln_t_kernel(h_ref, yt_ref, g_ref, b_ref, o_ref, ob_ref):
    out = _layer_norm(ALPHA * h_ref[...] + yt_ref[...].T, g_ref[...], b_ref[...])
    o_ref[...] = out
    ob_ref[...] = out.astype(BF16)


def _residual_ln_t(h, yt, g, b, tm=512):
    n, d = h.shape
    tm = min(tm, n)
    row = lambda i: (i, 0)
    fixed = lambda i: (0, 0)
    return pl.pallas_call(
        _residual_ln_t_kernel,
        grid=(n // tm,),
        in_specs=[pl.BlockSpec((tm, d), row), pl.BlockSpec((d, tm), lambda i: (0, i)),
                  pl.BlockSpec((1, d), fixed), pl.BlockSpec((1, d), fixed)],
        out_specs=[pl.BlockSpec((tm, d), row), pl.BlockSpec((tm, d), row)],
        out_shape=[jax.ShapeDtypeStruct((n, d), F32), jax.ShapeDtypeStruct((n, d), BF16)],
        compiler_params=_params("arbitrary"),
        name="residual_ln_t",
    )(h, yt, g.reshape(1, d), b.reshape(1, d))


def _peer_layer(h, hb, w_q, sub_keys, u_tab, v_tab, ln_g, ln_b):
    q = _matmul(hb, w_q.astype(BF16), BF16, name="peer_query")
    r2, c1, e1, e2 = _peer_route(q, sub_keys.astype(BF16))
    yt = _peer_experts(hb, u_tab.astype(BF16), v_tab.T.astype(BF16), r2, e2, c1, e1)
    return _residual_ln_t(h, yt, ln_g, ln_b)


def _sb_mixer(hb, batch, seq, w_in, w_out):
    qkv = _matmul(hb, w_in.astype(BF16), BF16, name="sb_in")
    return _sb_attention(qkv, batch, seq), w_out


def _ml_mixer(hb, batch, seq, w_in, b_gates, norm_g, w_out):
    heads = ML_HEADS
    d_gate = w_out.shape[0]
    n_qkv = w_in.shape[1] - d_gate - 2 * heads
    qkv = _matmul(hb, w_in[:, :n_qkv].astype(BF16), BF16, name="ml_in_qkv")
    w_og = jnp.pad(w_in[:, n_qkv:], ((0, 0), (0, LANES - 2 * heads))).astype(BF16)
    og = _matmul(hb, w_og, F32, tm=512, tn=w_og.shape[1], name="ml_in_gates")
    bias = jnp.pad(b_gates.astype(F32), (0, LANES - 2 * heads)).reshape(1, LANES)
    return _mlstm(qkv, og, bias, norm_g, batch, seq), w_out


def _sg_mixer(hb, batch, seq, w_in, norm_g, norm_b, w_s, b_s, w_out):
    uv = _matmul(hb, w_in.astype(BF16), F32, act="gelu", name="sg_in")
    return _sgu(uv, norm_g, norm_b, w_s, b_s), w_out


def kernel(x, l0_sb_w_in, l0_sb_w_out, l0_ln1_g, l0_ln1_b, l0_peer_w_q, l0_peer_sub_keys, l0_peer_u, l0_peer_v, l0_ln2_g, l0_ln2_b, l1_ml_w_in, l1_ml_b_gates, l1_ml_norm_g, l1_ml_w_out, l1_ln1_g, l1_ln1_b, l1_peer_w_q, l1_peer_sub_keys, l1_peer_u, l1_peer_v, l1_ln2_g, l1_ln2_b, l2_sg_w_in, l2_sg_norm_g, l2_sg_norm_b, l2_sg_w_s, l2_sg_b_s, l2_sg_w_out, l2_ln1_g, l2_ln1_b, l2_peer_w_q, l2_peer_sub_keys, l2_peer_u, l2_peer_v, l2_ln2_g, l2_ln2_b, l3_sb_w_in, l3_sb_w_out, l3_ln1_g, l3_ln1_b, l3_peer_w_q, l3_peer_sub_keys, l3_peer_u, l3_peer_v, l3_ln2_g, l3_ln2_b):
    batch, seq, d = x.shape
    mixers = (
        (_sb_mixer, (l0_sb_w_in, l0_sb_w_out)),
        (_ml_mixer, (l1_ml_w_in, l1_ml_b_gates, l1_ml_norm_g, l1_ml_w_out)),
        (_sg_mixer, (l2_sg_w_in, l2_sg_norm_g, l2_sg_norm_b, l2_sg_w_s, l2_sg_b_s, l2_sg_w_out)),
        (_sb_mixer, (l3_sb_w_in, l3_sb_w_out)),
    )
    norm1 = ((l0_ln1_g, l0_ln1_b), (l1_ln1_g, l1_ln1_b), (l2_ln1_g, l2_ln1_b), (l3_ln1_g, l3_ln1_b))
    peers = (
        (l0_peer_w_q, l0_peer_sub_keys, l0_peer_u, l0_peer_v, l0_ln2_g, l0_ln2_b),
        (l1_peer_w_q, l1_peer_sub_keys, l1_peer_u, l1_peer_v, l1_ln2_g, l1_ln2_b),
        (l2_peer_w_q, l2_peer_sub_keys, l2_peer_u, l2_peer_v, l2_ln2_g, l2_ln2_b),
        (l3_peer_w_q, l3_peer_sub_keys, l3_peer_u, l3_peer_v, l3_ln2_g, l3_ln2_b),
    )
    h = x.reshape(batch * seq, d)
    hb = h.astype(BF16)
    for (mixer, mixer_params), (g1, b1), peer_params in zip(mixers, norm1, peers):
        pre, w_out = mixer(hb, batch, seq, *mixer_params)
        h, hb = _proj_residual_ln(pre, w_out.astype(BF16), h, g1, b1)
        h, hb = _peer_layer(h, hb, *peer_params)
    return h.reshape(batch, seq, d)
```

```python
import functools

import jax
import jax.numpy as jnp
from jax import lax
from jax.experimental import pallas as pl
from jax.experimental.pallas import tpu as pltpu

F32 = jnp.float32
BF16 = jnp.bfloat16

LANES = 128
BF16_SUBLANES = 16
CHUNK = 128
SB_HEADS = 16
ML_HEADS = 8
SG_GROUPS = 8
PEER_HEADS = 8
PEER_KEYS = 128
PEER_TOPK = 16
DEPTH = 4
ALPHA = (2 * DEPTH) ** 0.25
LN_EPS = 1e-5
VMEM_LIMIT_BYTES = 56 * 1024 * 1024
EXP_ZERO_BELOW = -110.0
NOT_RANKED = 99.0


def _params(*sem):
    return pltpu.CompilerParams(dimension_semantics=sem, vmem_limit_bytes=VMEM_LIMIT_BYTES)


def _gelu(x):
    return 0.5 * x * (1.0 + lax.erf(x * 0.7071067811865476))


def _softplus(z):
    return jnp.maximum(z, 0.0) + jnp.log1p(jnp.exp(-jnp.abs(z)))


def _log_sigmoid(z):
    return -_softplus(-z)


def _layer_norm(t, g, b):
    mu = jnp.mean(t, axis=-1, keepdims=True)
    d = t - mu
    var = jnp.mean(d * d, axis=-1, keepdims=True)
    return d * lax.rsqrt(var + LN_EPS) * g + b


def _dot(a, b):
    return jnp.dot(a, b, preferred_element_type=F32)


def _dot_nt(a, b):
    return lax.dot_general(a, b, (((1,), (1,)), ((), ())), preferred_element_type=F32)


def _dot_tn(a, b):
    return lax.dot_general(a, b, (((0,), (0,)), ((), ())), preferred_element_type=F32)


def _mm_kernel(x_ref, w_ref, o_ref, *, act):
    y = _dot(x_ref[...], w_ref[...])
    if act == "gelu":
        y = _gelu(y)
    o_ref[...] = y.astype(o_ref.dtype)


def _matmul(x, w, out_dtype, act=None, tm=1024, tn=1024, name="matmul"):
    m, k = x.shape
    n = w.shape[1]
    tm, tn = min(tm, m), min(tn, n)
    assert m % tm == 0 and n % tn == 0, (m, n, tm, tn)
    return pl.pallas_call(
        functools.partial(_mm_kernel, act=act),
        grid=(n // tn, m // tm),
        in_specs=[pl.BlockSpec((tm, k), lambda j, i: (i, 0)),
                  pl.BlockSpec((k, tn), lambda j, i: (0, j))],
        out_specs=pl.BlockSpec((tm, tn), lambda j, i: (i, j)),
        out_shape=jax.ShapeDtypeStruct((m, n), out_dtype),
        compiler_params=_params("arbitrary", "arbitrary"),
        name=name,
    )(x, w)


def _proj_ln_kernel(x_ref, w_ref, h_ref, g_ref, b_ref, o_ref, ob_ref):
    y = _dot(x_ref[...], w_ref[...])
    out = _layer_norm(ALPHA * h_ref[...] + y, g_ref[...], b_ref[...])
    o_ref[...] = out
    ob_ref[...] = out.astype(BF16)


def _proj_residual_ln(x, w, h, g, b, tm=512):
    m, k = x.shape
    d = w.shape[1]
    tm = min(tm, m)
    assert m % tm == 0
    row = lambda i: (i, 0)
    fixed = lambda i: (0, 0)
    return pl.pallas_call(
        _proj_ln_kernel,
        grid=(m // tm,),
        in_specs=[pl.BlockSpec((tm, k), row), pl.BlockSpec((k, d), fixed),
                  pl.BlockSpec((tm, d), row), pl.BlockSpec((1, d), fixed),
                  pl.BlockSpec((1, d), fixed)],
        out_specs=[pl.BlockSpec((tm, d), row), pl.BlockSpec((tm, d), row)],
        out_shape=[jax.ShapeDtypeStruct((m, d), F32), jax.ShapeDtypeStruct((m, d), BF16)],
        compiler_params=_params("arbitrary"),
        name="proj_residual_ln",
    )(x, w, h, g.reshape(1, d), b.reshape(1, d))


def _sb_kernel(q_ref, k_ref, v_ref, o_ref, *, blk, win, dh, scale):
    qi = pl.program_id(2)
    heads = q_ref.shape[1] // dh
    row = lax.broadcasted_iota(jnp.int32, (win, win), 0)
    col = lax.broadcasted_iota(jnp.int32, (win, win), 1)
    suffix = jnp.concatenate(
        [(row > col).astype(BF16), jnp.ones((win, LANES), BF16)], axis=1)
    q_pos = qi * blk + lax.broadcasted_iota(jnp.int32, (blk, win), 0)
    offset = lax.broadcasted_iota(jnp.int32, (blk, win), 1)

    def window(start, valid, h, carry_sum, acc):
        cols = slice(h * dh, (h + 1) * dh)
        k = k_ref[pl.ds(start, win), cols]
        v = v_ref[pl.ds(start, win), cols]
        z = _dot_nt(q_ref[:, cols], k) * scale
        sp = _softplus(z)
        log_keep = jnp.where(valid, -sp, 0.0)
        hi = log_keep.astype(BF16)
        lo = (log_keep - hi.astype(F32)).astype(BF16)
        sums = _dot(hi, suffix) + _dot(lo, suffix)
        between = sums[:, :win] + jnp.concatenate([carry_sum] * (win // LANES), axis=1)
        w = jnp.where(valid, jnp.exp(z - sp + between), 0.0)
        return carry_sum + sums[:, win:], acc + _dot(w.astype(BF16), v)

    def cond(c):
        return jnp.logical_and(c[0] > 0, c[1] > EXP_ZERO_BELOW)

    def body(c):
        end, _, sums, accs = c
        start = pl.multiple_of(jnp.maximum(end - win, 0), blk)
        key_pos = start + offset
        valid = jnp.logical_and(key_pos < q_pos, key_pos < end)
        out = [window(start, valid, h, sums[h], accs[h]) for h in range(heads)]
        sums = tuple(o[0] for o in out)
        return start, jnp.max(functools.reduce(jnp.maximum, sums)), sums, tuple(o[1] for o in out)

    init = ((qi + 1) * blk, jnp.float32(0.0), (jnp.zeros((blk, LANES), F32),) * heads,
            (jnp.zeros((blk, dh), F32),) * heads)
    accs = lax.while_loop(cond, body, init)[3]
    for h in range(heads):
        o_ref[:, h * dh:(h + 1) * dh] = accs[h].astype(o_ref.dtype)


def _sb_attention(qkv, batch, seq, blk=CHUNK, win=3 * CHUNK, heads_per_step=8):
    n, three_d = qkv.shape
    d = three_d // 3
    dh = d // SB_HEADS
    groups = SB_HEADS // heads_per_step
    width = heads_per_step * dh
    nq = seq // blk
    win = min(win, seq)
    return pl.pallas_call(
        functools.partial(_sb_kernel, blk=blk, win=win, dh=dh, scale=dh ** -0.5),
        grid=(batch, groups, nq),
        in_specs=[pl.BlockSpec((blk, width), lambda b, g, i: (b * nq + i, g)),
                  pl.BlockSpec((seq, width), lambda b, g, i: (b, groups + g)),
                  pl.BlockSpec((seq, width), lambda b, g, i: (b, 2 * groups + g))],
        out_specs=pl.BlockSpec((blk, width), lambda b, g, i: (b * nq + i, g)),
        out_shape=jax.ShapeDtypeStruct((n, d), BF16),
        compiler_params=_params("arbitrary", "arbitrary", "arbitrary"),
        name="sb_attention",
    )(qkv, qkv, qkv)


def _mlstm_kernel(q_ref, k_ref, v_ref, o_ref, g_ref, bias_ref, ng_ref, y_ref, c_ref, m_ref,
                  *, dk, dv, scale):
    L = q_ref.shape[0]
    heads = ML_HEADS

    @pl.when(pl.program_id(1) == 0)
    def _():
        c_ref[...] = jnp.zeros_like(c_ref)
        m_ref[...] = jnp.zeros_like(m_ref)

    g = g_ref[...] + bias_ref[...]
    log_f = _log_sigmoid(g)
    row = lax.broadcasted_iota(jnp.int32, (L, L), 0)
    col = lax.broadcasted_iota(jnp.int32, (L, L), 1)
    causal = col <= row
    b_cols = jnp.dot(causal.astype(F32), log_f, precision=lax.Precision.HIGHEST,
                     preferred_element_type=F32)
    b_rows = b_cols.T
    g_rows = g.T
    ones_col = (lax.broadcasted_iota(jnp.int32, (L, LANES), 1) == 0).astype(BF16)

    for h in range(heads):
        q = q_ref[:, h * dk:(h + 1) * dk]
        k = k_ref[:, h * dk:(h + 1) * dk]
        v_ext = jnp.concatenate([v_ref[:, h * dv:(h + 1) * dv], ones_col], axis=1)
        li_col = g[:, h:h + 1]
        li_row = g_rows[h:h + 1, :]
        b_col = b_cols[:, heads + h:heads + h + 1]
        b_row = b_rows[heads + h:heads + h + 1, :]
        m_prev = m_ref[h:h + 1, 0:1]
        c_ext = c_ref[h]

        d_intra = jnp.where(causal, b_col - b_row + li_row, -jnp.inf)
        d_inter = b_col + m_prev
        m_t = jnp.maximum(d_inter, jnp.max(d_intra, axis=-1, keepdims=True))
        w_intra = jnp.exp(d_intra - m_t)
        w_inter = jnp.exp(d_inter - m_t)
        s = _dot_nt(q, k) * scale * w_intra
        num = _dot(s.astype(BF16), v_ext) + w_inter * (_dot(q, c_ext.astype(BF16)) * scale)
        den = num[:, dv:dv + 1]
        hid = num[:, :dv] / jnp.maximum(jnp.abs(den), jnp.exp(-m_t))

        b_last = b_col[L - 1:L, :]
        m_new = jnp.maximum(b_last + m_prev,
                            jnp.max(b_last - b_row + li_row, axis=-1, keepdims=True))
        decay = jnp.exp(b_last + m_prev - m_new)
        w_state = jnp.exp(b_last - b_col + li_col - m_new)
        kw = (k.astype(F32) * w_state).astype(BF16)
        c_ref[h] = decay * c_ext + _dot_tn(kw, v_ext)
        m_ref[h:h + 1, :] = jnp.broadcast_to(m_new, (1, LANES))

        hid = hid * lax.rsqrt(jnp.mean(hid * hid, axis=-1, keepdims=True) + LN_EPS)
        hid = hid * ng_ref[:, h * dv:(h + 1) * dv]
        gate = jax.nn.sigmoid(o_ref[:, h * dv:(h + 1) * dv])
        y_ref[:, h * dv:(h + 1) * dv] = (gate * hid).astype(y_ref.dtype)


def _mlstm(qkv, og, bias, norm_g, batch, seq):
    n = qkv.shape[0]
    heads = ML_HEADS
    d = og.shape[1] - LANES
    dv = d // heads
    dk = dv // 2
    L = CHUNK
    nc = seq // L
    tok = lambda b, c: (b * nc + c, 0)
    fixed = lambda b, c: (0, 0)
    return pl.pallas_call(
        functools.partial(_mlstm_kernel, dk=dk, dv=dv, scale=dk ** -0.5),
        grid=(batch, nc),
        in_specs=[pl.BlockSpec((L, heads * dk), tok),
                  pl.BlockSpec((L, heads * dk), lambda b, c: (b * nc + c, 1)),
                  pl.BlockSpec((L, d), lambda b, c: (b * nc + c, 1)),
                  pl.BlockSpec((L, d), tok),
                  pl.BlockSpec((L, LANES), lambda b, c: (b * nc + c, d // LANES)),
                  pl.BlockSpec((1, LANES), fixed),
                  pl.BlockSpec((1, d), fixed)],
        out_specs=pl.BlockSpec((L, d), tok),
        out_shape=jax.ShapeDtypeStruct((n, d), BF16),
        scratch_shapes=[pltpu.VMEM((heads, dk, dv + LANES), F32),
                        pltpu.VMEM((heads, LANES), F32)],
        compiler_params=_params("arbitrary", "arbitrary"),
        name="mlstm",
    )(qkv, qkv, qkv, og, og, bias, norm_g.reshape(1, d))


def _sgu_kernel(u_ref, v_ref, g_ref, b_ref, ws_ref, bs_ref, y_ref):
    L, width = u_ref.shape
    dg = width // SG_GROUPS
    vn = _layer_norm(v_ref[...], g_ref[...], b_ref[...]).astype(BF16)
    row = lax.broadcasted_iota(jnp.int32, (L, L), 0)
    col = lax.broadcasted_iota(jnp.int32, (L, L), 1)
    for g in range(SG_GROUPS):
        ws = jnp.where(col <= row, ws_ref[g], 0.0).astype(BF16)
        mixed = _dot(ws, vn[:, g * dg:(g + 1) * dg]) + bs_ref[:, g:g + 1]
        y_ref[:, g * dg:(g + 1) * dg] = (u_ref[:, g * dg:(g + 1) * dg] * mixed).astype(y_ref.dtype)


def _sgu(uv, norm_g, norm_b, w_s, b_s):
    n, two_w = uv.shape
    width = two_w // 2
    L = CHUNK
    bs_t = jnp.zeros((L, LANES), F32).at[:, :SG_GROUPS].set(b_s.T)
    fixed = lambda i: (0, 0)
    return pl.pallas_call(
        _sgu_kernel,
        grid=(n // L,),
        in_specs=[pl.BlockSpec((L, width), lambda i: (i, 0)),
                  pl.BlockSpec((L, width), lambda i: (i, 1)),
                  pl.BlockSpec((1, width), fixed), pl.BlockSpec((1, width), fixed),
                  pl.BlockSpec((SG_GROUPS, L, L), lambda i: (0, 0, 0)),
                  pl.BlockSpec((L, LANES), fixed)],
        out_specs=pl.BlockSpec((L, width), lambda i: (i, 0)),
        out_shape=jax.ShapeDtypeStruct((n, width), BF16),
        compiler_params=_params("arbitrary"),
        name="sgu",
    )(uv, uv, norm_g.reshape(1, width), norm_b.reshape(1, width), w_s, bs_t)


def _extract_top(x, count, first_only):
    rows = x.shape[0]
    iota = lax.broadcasted_iota(jnp.int32, x.shape, 0)
    rank = jnp.full(x.shape, NOT_RANKED, F32)
    vals = []
    for k in range(count):
        mx = jnp.max(x, axis=0, keepdims=True)
        sel = x == mx
        if first_only:
            sel = iota == jnp.min(jnp.where(sel, iota, rows), axis=0, keepdims=True)
        vals.append(mx)
        rank = jnp.where(sel, float(k + 1), rank)
        x = jnp.where(sel, -jnp.inf, x)
    return vals, rank


def _count_ranked(rank):
    return jnp.sum((rank < NOT_RANKED).astype(F32), axis=0, keepdims=True)


def _route_one_head(s1, s2, first_only):
    K = PEER_TOPK
    T = s1.shape[1]
    a, rank1 = _extract_top(s1, K, first_only)
    b, rank2 = _extract_top(s2, K, first_only)

    b_all = jnp.concatenate(b, axis=0)
    sub = lax.broadcasted_iota(jnp.int32, (8, T), 0)
    blocks = [a[0] + b_all]
    for p in range(1, K):
        blocks.append(jnp.where(sub < K // (p + 1), a[p] + b_all[:8], -jnp.inf))
    cand = jnp.concatenate(blocks, axis=0)
    best, order = _extract_top(cand, K, first_only)
    chosen = (order < NOT_RANKED).astype(F32)
    counts = [jnp.sum(chosen[:K], axis=0, keepdims=True)]
    for p in range(1, K):
        counts.append(jnp.sum(chosen[K + 8 * (p - 1):K + 8 * p], axis=0, keepdims=True))
    ranked = jnp.maximum(jnp.maximum(_count_ranked(rank1), _count_ranked(rank2)),
                         functools.reduce(jnp.add, counts))

    z = jnp.zeros_like(best[0])
    for k in range(K):
        z = z + jnp.exp(best[k] - best[0])
    c1 = jnp.zeros_like(s1)
    for p in range(K):
        c1 = jnp.where(rank1 == float(p + 1), counts[p], c1)
    return rank2, c1, jnp.exp(s1 - a[0]) / z, jnp.exp(s2 - b[0]), ranked


def _peer_route_kernel(q_ref, keys_ref, r2_ref, c1_ref, e1_ref, e2_ref):
    half = keys_ref.shape[3]

    def route(first_only):
        ranked = []
        for h in range(keys_ref.shape[0]):
            s1 = _dot_nt(keys_ref[h, 0], q_ref[:, 2 * h * half:(2 * h + 1) * half])
            s2 = _dot_nt(keys_ref[h, 1], q_ref[:, (2 * h + 1) * half:(2 * h + 2) * half])
            r2, c1_ref[h], e1_ref[h], e2, n = _route_one_head(s1, s2, first_only)
            r2_ref[h] = r2.astype(r2_ref.dtype)
            e2_ref[h] = e2.astype(e2_ref.dtype)
            ranked.append(n)
        return jnp.max(functools.reduce(jnp.maximum, ranked))

    most_ranked = route(False)

    @pl.when(most_ranked > PEER_TOPK)
    def _():
        route(True)


def _peer_route(q, keys, tt=LANES, heads_per_step=4):
    n = q.shape[0]
    heads, _, nk, half = keys.shape
    hp = heads_per_step
    out = lambda dtype: jax.ShapeDtypeStruct((heads, nk, n), dtype)
    spec = pl.BlockSpec((hp, nk, tt), lambda i, g: (g, 0, i))
    return pl.pallas_call(
        _peer_route_kernel,
        grid=(n // tt, heads // hp),
        in_specs=[pl.BlockSpec((tt, 2 * half * hp), lambda i, g: (i, g)),
                  pl.BlockSpec((hp, 2, nk, half), lambda i, g: (g, 0, 0, 0))],
        out_specs=[spec] * 4,
        out_shape=[out(BF16), out(F32), out(F32), out(BF16)],
        compiler_params=_params("arbitrary", "arbitrary"),
        name="peer_route",
    )(q, keys)


def _peer_expert_kernel(x_ref, u_ref, vta_ref, vtb_ref, r2_ref, e2_ref, c1_ref, e1_ref, o_ref,
                        pa_ref, pb_ref, *, lane_chunk, sub_experts):
    e = pl.program_id(1)
    last = pl.num_programs(1) - 1
    nk = PEER_KEYS
    tb = x_ref.shape[0]
    half = u_ref.shape[0] // 2
    n_sub = half // sub_experts
    d_rows = vta_ref.shape[0] // n_sub

    def rows_to_tile(ref, h, ii, ts):
        row = jnp.broadcast_to(ref[h, ii:ii + 1, ts], (BF16_SUBLANES, lane_chunk)).astype(BF16)
        return jnp.concatenate([row] * (nk // BF16_SUBLANES), axis=0)

    def evaluate(p_ref, first, c):
        rows = slice(first + c * sub_experts, first + (c + 1) * sub_experts)
        act = _gelu(_dot_nt(u_ref[rows, :], x_ref[...])).astype(BF16)
        for r0 in range(0, sub_experts, nk):
            ii = (rows.start + r0) // nk
            for t0 in range(0, tb, lane_chunk):
                ts = slice(t0, t0 + lane_chunk)
                gate = jnp.zeros((nk, lane_chunk), BF16)
                for h in range(PEER_HEADS):
                    picked = r2_ref[h, :, ts] <= rows_to_tile(c1_ref, h, ii, ts)
                    gate = gate + jnp.where(picked, e2_ref[h, :, ts], 0.0) * rows_to_tile(e1_ref, h, ii, ts)
                p_ref[c * sub_experts + r0:c * sub_experts + r0 + nk, ts] = gate * act[r0:r0 + nk, ts]

    def apply(vt_ref, p_ref, c):
        rows = slice(c * d_rows, (c + 1) * d_rows)
        o_ref[rows, :] += _dot(vt_ref[rows, :], p_ref[...])

    @pl.when(e == 0)
    def _():
        o_ref[...] = jnp.zeros_like(o_ref)
        pb_ref[...] = jnp.zeros_like(pb_ref)

    @pl.when(e < last)
    def _():
        for c in range(n_sub):
            apply(vta_ref, pb_ref, c)
            evaluate(pa_ref, 0, c)
        for c in range(n_sub):
            apply(vtb_ref, pa_ref, c)
            evaluate(pb_ref, half, c)

    @pl.when(e == last)
    def _():
        for c in range(n_sub):
            apply(vta_ref, pb_ref, c)


def _peer_experts(x, u, vt, r2, e2, c1, e1, tb=1024, eb=1024, lane_chunk=256, sub_experts=256):
    n, d = x.shape
    ne = u.shape[0] // eb
    heads, nk, _ = r2.shape
    tb = min(tb, n)
    rows_i = eb // nk
    half = eb // 2
    tok = lambda t, e: (0, 0, t)
    cur = lambda e: jnp.minimum(e, ne - 1)
    return pl.pallas_call(
        functools.partial(_peer_expert_kernel, lane_chunk=min(lane_chunk, tb), sub_experts=sub_experts),
        grid=(n // tb, ne + 1),
        in_specs=[pl.BlockSpec((tb, d), lambda t, e: (t, 0)),
                  pl.BlockSpec((eb, d), lambda t, e: (cur(e), 0)),
                  pl.BlockSpec((d, half), lambda t, e: (0, jnp.maximum(2 * e - 1, 0))),
                  pl.BlockSpec((d, half), lambda t, e: (0, 2 * cur(e))),
                  pl.BlockSpec((heads, nk, tb), tok),
                  pl.BlockSpec((heads, nk, tb), tok),
                  pl.BlockSpec((heads, rows_i, tb), lambda t, e: (0, cur(e), t)),
                  pl.BlockSpec((heads, rows_i, tb), lambda t, e: (0, cur(e), t))],
        out_specs=pl.BlockSpec((d, tb), lambda t, e: (0, t)),
        out_shape=jax.ShapeDtypeStruct((d, n), F32),
        scratch_shapes=[pltpu.VMEM((half, tb), BF16), pltpu.VMEM((half, tb), BF16)],
        compiler_params=_params("arbitrary", "arbitrary"),
        name="peer_experts",
    )(x, u, vt, vt, r2, e2, c1, e1)


def _residual_ln_t_kernel(h_ref, yt_ref, g_ref, b_ref, o_ref, ob_ref):
    out = _layer_norm(ALPHA * h_ref[...] + yt_ref[...].T, g_ref[...], b_ref[...])
    o_ref[...] = out
    ob_ref[...] = out.astype(BF16)


def _residual_ln_t(h, yt, g, b, tm=512):
    n, d = h.shape
    tm = min(tm, n)
    row = lambda i: (i, 0)
    fixed = lambda i: (0, 0)
    return pl.pallas_call(
        _residual_ln_t_kernel,
        grid=(n // tm,),
        in_specs=[pl.BlockSpec((tm, d), row), pl.BlockSpec((d, tm), lambda i: (0, i)),
                  pl.BlockSpec((1, d), fixed), pl.BlockSpec((1, d), fixed)],
        out_specs=[pl.BlockSpec((tm, d), row), pl.BlockSpec((tm, d), row)],
        out_shape=[jax.ShapeDtypeStruct((n, d), F32), jax.ShapeDtypeStruct((n, d), BF16)],
        compiler_params=_params("arbitrary"),
        name="residual_ln_t",
    )(h, yt, g.reshape(1, d), b.reshape(1, d))


def _peer_layer(h, hb, w_q, sub_keys, u_tab, v_tab, ln_g, ln_b):
    q = _matmul(hb, w_q.astype(BF16), BF16, name="peer_query")
    r2, c1, e1, e2 = _peer_route(q, sub_keys.astype(BF16))
    yt = _peer_experts(hb, u_tab.astype(BF16), v_tab.T.astype(BF16), r2, e2, c1, e1)
    return _residual_ln_t(h, yt, ln_g, ln_b)


def _sb_mixer(hb, batch, seq, w_in, w_out):
    qkv = _matmul(hb, w_in.astype(BF16), BF16, name="sb_in")
    return _sb_attention(qkv, batch, seq), w_out


def _ml_mixer(hb, batch, seq, w_in, b_gates, norm_g, w_out):
    heads = ML_HEADS
    d_gate = w_out.shape[0]
    n_qkv = w_in.shape[1] - d_gate - 2 * heads
    qkv = _matmul(hb, w_in[:, :n_qkv].astype(BF16), BF16, name="ml_in_qkv")
    w_og = jnp.pad(w_in[:, n_qkv:], ((0, 0), (0, LANES - 2 * heads))).astype(BF16)
    og = _matmul(hb, w_og, F32, tm=512, tn=w_og.shape[1], name="ml_in_gates")
    bias = jnp.pad(b_gates.astype(F32), (0, LANES - 2 * heads)).reshape(1, LANES)
    return _mlstm(qkv, og, bias, norm_g, batch, seq), w_out


def _sg_mixer(hb, batch, seq, w_in, norm_g, norm_b, w_s, b_s, w_out):
    uv = _matmul(hb, w_in.astype(BF16), F32, act="gelu", name="sg_in")
    return _sgu(uv, norm_g, norm_b, w_s, b_s), w_out


def kernel(x, l0_sb_w_in, l0_sb_w_out, l0_ln1_g, l0_ln1_b, l0_peer_w_q, l0_peer_sub_keys, l0_peer_u, l0_peer_v, l0_ln2_g, l0_ln2_b, l1_ml_w_in, l1_ml_b_gates, l1_ml_norm_g, l1_ml_w_out, l1_ln1_g, l1_ln1_b, l1_peer_w_q, l1_peer_sub_keys, l1_peer_u, l1_peer_v, l1_ln2_g, l1_ln2_b, l2_sg_w_in, l2_sg_norm_g, l2_sg_norm_b, l2_sg_w_s, l2_sg_b_s, l2_sg_w_out, l2_ln1_g, l2_ln1_b, l2_peer_w_q, l2_peer_sub_keys, l2_peer_u, l2_peer_v, l2_ln2_g, l2_ln2_b, l3_sb_w_in, l3_sb_w_out, l3_ln1_g, l3_ln1_b, l3_peer_w_q, l3_peer_sub_keys, l3_peer_u, l3_peer_v, l3_ln2_g, l3_ln2_b):
    batch, seq, d = x.shape
    mixers = (
        (_sb_mixer, (l0_sb_w_in, l0_sb_w_out)),
        (_ml_mixer, (l1_ml_w_in, l1_ml_b_gates, l1_ml_norm_g, l1_ml_w_out)),
        (_sg_mixer, (l2_sg_w_in, l2_sg_norm_g, l2_sg_norm_b, l2_sg_w_s, l2_sg_b_s, l2_sg_w_out)),
        (_sb_mixer, (l3_sb_w_in, l3_sb_w_out)),
    )
    norm1 = ((l0_ln1_g, l0_ln1_b), (l1_ln1_g, l1_ln1_b), (l2_ln1_g, l2_ln1_b), (l3_ln1_g, l3_ln1_b))
    peers = (
        (l0_peer_w_q, l0_peer_sub_keys, l0_peer_u, l0_peer_v, l0_ln2_g, l0_ln2_b),
        (l1_peer_w_q, l1_peer_sub_keys, l1_peer_u, l1_peer_v, l1_ln2_g, l1_ln2_b),
        (l2_peer_w_q, l2_peer_sub_keys, l2_peer_u, l2_peer_v, l2_ln2_g, l2_ln2_b),
        (l3_peer_w_q, l3_peer_sub_keys, l3_peer_u, l3_peer_v, l3_ln2_g, l3_ln2_b),
    )
    h = x.reshape(batch * seq, d)
    hb = h.astype(BF16)
    for (mixer, mixer_params), (g1, b1), peer_params in zip(mixers, norm1, peers):
        pre, w_out = mixer(hb, batch, seq, *mixer_params)
        h, hb = _proj_residual_ln(pre, w_out.astype(BF16), h, g1, b1)
        h, hb = _peer_layer(h, hb, *peer_params)
    return h.reshape(batch, seq, d)
```

```python
import functools

import jax
import jax.numpy as jnp
from jax import lax
from jax.experimental import pallas as pl
from jax.experimental.pallas import tpu as pltpu

F32 = jnp.float32
BF16 = jnp.bfloat16

LANES = 128
BF16_SUBLANES = 16
CHUNK = 128
SB_HEADS = 16
ML_HEADS = 8
SG_GROUPS = 8
PEER_HEADS = 8
PEER_KEYS = 128
PEER_TOPK = 16
DEPTH = 4
ALPHA = (2 * DEPTH) ** 0.25
LN_EPS = 1e-5
VMEM_LIMIT_BYTES = 56 * 1024 * 1024
EXP_ZERO_BELOW = -110.0
NOT_RANKED = 99.0


def _params(*sem):
    return pltpu.CompilerParams(dimension_semantics=sem, vmem_limit_bytes=VMEM_LIMIT_BYTES)


def _gelu(x):
    return 0.5 * x * (1.0 + lax.erf(x * 0.7071067811865476))


def _softplus(z):
    return jnp.maximum(z, 0.0) + jnp.log1p(jnp.exp(-jnp.abs(z)))


def _log_sigmoid(z):
    return -_softplus(-z)


def _layer_norm(t, g, b):
    mu = jnp.mean(t, axis=-1, keepdims=True)
    d = t - mu
    var = jnp.mean(d * d, axis=-1, keepdims=True)
    return d * lax.rsqrt(var + LN_EPS) * g + b


def _dot(a, b):
    return jnp.dot(a, b, preferred_element_type=F32)


def _dot_nt(a, b):
    return lax.dot_general(a, b, (((1,), (1,)), ((), ())), preferred_element_type=F32)


def _dot_tn(a, b):
    return lax.dot_general(a, b, (((0,), (0,)), ((), ())), preferred_element_type=F32)


def _mm_kernel(x_ref, w_ref, o_ref, *, act):
    y = _dot(x_ref[...], w_ref[...])
    if act == "gelu":
        y = _gelu(y)
    o_ref[...] = y.astype(o_ref.dtype)


def _matmul(x, w, out_dtype, act=None, tm=1024, tn=1024, name="matmul"):
    m, k = x.shape
    n = w.shape[1]
    tm, tn = min(tm, m), min(tn, n)
    assert m % tm == 0 and n % tn == 0, (m, n, tm, tn)
    return pl.pallas_call(
        functools.partial(_mm_kernel, act=act),
        grid=(n // tn, m // tm),
        in_specs=[pl.BlockSpec((tm, k), lambda j, i: (i, 0)),
                  pl.BlockSpec((k, tn), lambda j, i: (0, j))],
        out_specs=pl.BlockSpec((tm, tn), lambda j, i: (i, j)),
        out_shape=jax.ShapeDtypeStruct((m, n), out_dtype),
        compiler_params=_params("arbitrary", "arbitrary"),
        name=name,
    )(x, w)


def _proj_ln_kernel(x_ref, w_ref, h_ref, g_ref, b_ref, o_ref, ob_ref):
    y = _dot(x_ref[...], w_ref[...])
    out = _layer_norm(ALPHA * h_ref[...] + y, g_ref[...], b_ref[...])
    o_ref[...] = out
    ob_ref[...] = out.astype(BF16)


def _proj_residual_ln(x, w, h, g, b, tm=512):
    m, k = x.shape
    d = w.shape[1]
    tm = min(tm, m)
    assert m % tm == 0
    row = lambda i: (i, 0)
    fixed = lambda i: (0, 0)
    return pl.pallas_call(
        _proj_ln_kernel,
        grid=(m // tm,),
        in_specs=[pl.BlockSpec((tm, k), row), pl.BlockSpec((k, d), fixed),
                  pl.BlockSpec((tm, d), row), pl.BlockSpec((1, d), fixed),
                  pl.BlockSpec((1, d), fixed)],
        out_specs=[pl.BlockSpec((tm, d), row), pl.BlockSpec((tm, d), row)],
        out_shape=[jax.ShapeDtypeStruct((m, d), F32), jax.ShapeDtypeStruct((m, d), BF16)],
        compiler_params=_params("arbitrary"),
        name="proj_residual_ln",
    )(x, w, h, g.reshape(1, d), b.reshape(1, d))


def _sb_kernel(q_ref, k_ref, v_ref, o_ref, *, blk, win, dh, scale):
    qi = pl.program_id(2)
    heads = q_ref.shape[1] // dh
    row = lax.broadcasted_iota(jnp.int32, (win, win), 0)
    col = lax.broadcasted_iota(jnp.int32, (win, win), 1)
    suffix = jnp.concatenate(
        [(row > col).astype(BF16), jnp.ones((win, LANES), BF16)], axis=1)
    q_pos = qi * blk + lax.broadcasted_iota(jnp.int32, (blk, win), 0)
    offset = lax.broadcasted_iota(jnp.int32, (blk, win), 1)

    def cond(c):
        return jnp.logical_and(c[0] > 0, c[1] > EXP_ZERO_BELOW)

    def body(c):
        end, _, carries, accs = c
        start = pl.multiple_of(jnp.maximum(end - win, 0), blk)
        key_pos = start + offset
        valid = jnp.logical_and(key_pos < q_pos, key_pos < end)
        log_beta, log_keep = [], []
        for h in range(heads):
            cols = slice(h * dh, (h + 1) * dh)
            z = _dot_nt(q_ref[:, cols], k_ref[pl.ds(start, win), cols]) * scale
            sp = _softplus(z)
            log_beta.append(z - sp)
            log_keep.append(jnp.where(valid, -sp, 0.0))
        keep = jnp.concatenate(log_keep, axis=0)
        hi = keep.astype(BF16)
        lo = (keep - hi.astype(F32)).astype(BF16)
        sums = _dot(jnp.concatenate([hi, lo], axis=0), suffix)
        sums = sums[:heads * blk] + sums[heads * blk:]
        new_carries, new_accs = [], []
        for h in range(heads):
            cols = slice(h * dh, (h + 1) * dh)
            part = sums[h * blk:(h + 1) * blk]
            between = part[:, :win] + jnp.concatenate([carries[h]] * (win // LANES), axis=1)
            w = jnp.where(valid, jnp.exp(log_beta[h] + between), 0.0)
            new_carries.append(carries[h] + part[:, win:])
            new_accs.append(accs[h] + _dot(w.astype(BF16), v_ref[pl.ds(start, win), cols]))
        largest = jnp.max(functools.reduce(jnp.maximum, new_carries))
        return start, largest, tuple(new_carries), tuple(new_accs)

    init = ((qi + 1) * blk, jnp.float32(0.0), (jnp.zeros((blk, LANES), F32),) * heads,
            (jnp.zeros((blk, dh), F32),) * heads)
    accs = lax.while_loop(cond, body, init)[3]
    for h in range(heads):
        o_ref[:, h * dh:(h + 1) * dh] = accs[h].astype(o_ref.dtype)


def _sb_attention(qkv, batch, seq, blk=CHUNK, win=3 * CHUNK, heads_per_step=8):
    n, three_d = qkv.shape
    d = three_d // 3
    dh = d // SB_HEADS
    groups = SB_HEADS // heads_per_step
    width = heads_per_step * dh
    nq = seq // blk
    win = min(win, seq)
    return pl.pallas_call(
        functools.partial(_sb_kernel, blk=blk, win=win, dh=dh, scale=dh ** -0.5),
        grid=(batch, groups, nq),
        in_specs=[pl.BlockSpec((blk, width), lambda b, g, i: (b * nq + i, g)),
                  pl.BlockSpec((seq, width), lambda b, g, i: (b, groups + g)),
                  pl.BlockSpec((seq, width), lambda b, g, i: (b, 2 * groups + g))],
        out_specs=pl.BlockSpec((blk, width), lambda b, g, i: (b * nq + i, g)),
        out_shape=jax.ShapeDtypeStruct((n, d), BF16),
        compiler_params=_params("arbitrary", "arbitrary", "arbitrary"),
        name="sb_attention",
    )(qkv, qkv, qkv)


def _mlstm_kernel(q_ref, k_ref, v_ref, o_ref, g_ref, bias_ref, ng_ref, y_ref, c_ref, m_ref,
                  *, dk, dv, scale):
    L = q_ref.shape[0]
    heads = ML_HEADS

    @pl.when(pl.program_id(1) == 0)
    def _():
        c_ref[...] = jnp.zeros_like(c_ref)
        m_ref[...] = jnp.zeros_like(m_ref)

    g = g_ref[...] + bias_ref[...]
    log_f = _log_sigmoid(g)
    row = lax.broadcasted_iota(jnp.int32, (L, L), 0)
    col = lax.broadcasted_iota(jnp.int32, (L, L), 1)
    causal = col <= row
    b_cols = jnp.dot(causal.astype(F32), log_f, precision=lax.Precision.HIGHEST,
                     preferred_element_type=F32)
    b_rows = b_cols.T
    g_rows = g.T
    ones_col = (lax.broadcasted_iota(jnp.int32, (L, LANES), 1) == 0).astype(BF16)

    for h in range(heads):
        q = q_ref[:, h * dk:(h + 1) * dk]
        k = k_ref[:, h * dk:(h + 1) * dk]
        v_ext = jnp.concatenate([v_ref[:, h * dv:(h + 1) * dv], ones_col], axis=1)
        li_col = g[:, h:h + 1]
        li_row = g_rows[h:h + 1, :]
        b_col = b_cols[:, heads + h:heads + h + 1]
        b_row = b_rows[heads + h:heads + h + 1, :]
        m_prev = m_ref[h:h + 1, 0:1]
        c_ext = c_ref[h]

        d_intra = jnp.where(causal, b_col - b_row + li_row, -jnp.inf)
        d_inter = b_col + m_prev
        m_t = jnp.maximum(d_inter, jnp.max(d_intra, axis=-1, keepdims=True))
        w_intra = jnp.exp(d_intra - m_t)
        w_inter = jnp.exp(d_inter - m_t)
        s = _dot_nt(q, k) * scale * w_intra
        num = _dot(s.astype(BF16), v_ext) + w_inter * (_dot(q, c_ext.astype(BF16)) * scale)
        den = num[:, dv:dv + 1]
        hid = num[:, :dv] / jnp.maximum(jnp.abs(den), jnp.exp(-m_t))

        b_last = b_col[L - 1:L, :]
        m_new = jnp.maximum(b_last + m_prev,
                            jnp.max(b_last - b_row + li_row, axis=-1, keepdims=True))
        decay = jnp.exp(b_last + m_prev - m_new)
        w_state = jnp.exp(b_last - b_col + li_col - m_new)
        kw = (k.astype(F32) * w_state).astype(BF16)
        c_ref[h] = decay * c_ext + _dot_tn(kw, v_ext)
        m_ref[h:h + 1, :] = jnp.broadcast_to(m_new, (1, LANES))

        hid = hid * lax.rsqrt(jnp.mean(hid * hid, axis=-1, keepdims=True) + LN_EPS)
        hid = hid * ng_ref[:, h * dv:(h + 1) * dv]
        gate = jax.nn.sigmoid(o_ref[:, h * dv:(h + 1) * dv])
        y_ref[:, h * dv:(h + 1) * dv] = (gate * hid).astype(y_ref.dtype)


def _mlstm(qkv, og, bias, norm_g, batch, seq):
    n = qkv.shape[0]
    heads = ML_HEADS
    d = og.shape[1] - LANES
    dv = d // heads
    dk = dv // 2
    L = CHUNK
    nc = seq // L
    tok = lambda b, c: (b * nc + c, 0)
    fixed = lambda b, c: (0, 0)
    return pl.pallas_call(
        functools.partial(_mlstm_kernel, dk=dk, dv=dv, scale=dk ** -0.5),
        grid=(batch, nc),
        in_specs=[pl.BlockSpec((L, heads * dk), tok),
                  pl.BlockSpec((L, heads * dk), lambda b, c: (b * nc + c, 1)),
                  pl.BlockSpec((L, d), lambda b, c: (b * nc + c, 1)),
                  pl.BlockSpec((L, d), tok),
                  pl.BlockSpec((L, LANES), lambda b, c: (b * nc + c, d // LANES)),
                  pl.BlockSpec((1, LANES), fixed),
                  pl.BlockSpec((1, d), fixed)],
        out_specs=pl.BlockSpec((L, d), tok),
        out_shape=jax.ShapeDtypeStruct((n, d), BF16),
        scratch_shapes=[pltpu.VMEM((heads, dk, dv + LANES), F32),
                        pltpu.VMEM((heads, LANES), F32)],
        compiler_params=_params("arbitrary", "arbitrary"),
        name="mlstm",
    )(qkv, qkv, qkv, og, og, bias, norm_g.reshape(1, d))


def _sgu_kernel(u_ref, v_ref, g_ref, b_ref, ws_ref, bs_ref, y_ref):
    L, width = u_ref.shape
    dg = width // SG_GROUPS
    vn = _layer_norm(v_ref[...], g_ref[...], b_ref[...]).astype(BF16)
    row = lax.broadcasted_iota(jnp.int32, (L, L), 0)
    col = lax.broadcasted_iota(jnp.int32, (L, L), 1)
    for g in range(SG_GROUPS):
        ws = jnp.where(col <= row, ws_ref[g], 0.0).astype(BF16)
        mixed = _dot(ws, vn[:, g * dg:(g + 1) * dg]) + bs_ref[:, g:g + 1]
        y_ref[:, g * dg:(g + 1) * dg] = (u_ref[:, g * dg:(g + 1) * dg] * mixed).astype(y_ref.dtype)


def _sgu(uv, norm_g, norm_b, w_s, b_s):
    n, two_w = uv.shape
    width = two_w // 2
    L = CHUNK
    bs_t = jnp.zeros((L, LANES), F32).at[:, :SG_GROUPS].set(b_s.T)
    fixed = lambda i: (0, 0)
    return pl.pallas_call(
        _sgu_kernel,
        grid=(n // L,),
        in_specs=[pl.BlockSpec((L, width), lambda i: (i, 0)),
                  pl.BlockSpec((L, width), lambda i: (i, 1)),
                  pl.BlockSpec((1, width), fixed), pl.BlockSpec((1, width), fixed),
                  pl.BlockSpec((SG_GROUPS, L, L), lambda i: (0, 0, 0)),
                  pl.BlockSpec((L, LANES), fixed)],
        out_specs=pl.BlockSpec((L, width), lambda i: (i, 0)),
        out_shape=jax.ShapeDtypeStruct((n, width), BF16),
        compiler_params=_params("arbitrary"),
        name="sgu",
    )(uv, uv, norm_g.reshape(1, width), norm_b.reshape(1, width), w_s, bs_t)


def _extract_top(x, count, first_only):
    rows = x.shape[0]
    iota = lax.broadcasted_iota(jnp.int32, x.shape, 0)
    rank = jnp.full(x.shape, NOT_RANKED, F32)
    vals = []
    for k in range(count):
        mx = jnp.max(x, axis=0, keepdims=True)
        sel = x == mx
        if first_only:
            sel = iota == jnp.min(jnp.where(sel, iota, rows), axis=0, keepdims=True)
        vals.append(mx)
        rank = jnp.where(sel, float(k + 1), rank)
        x = jnp.where(sel, -jnp.inf, x)
    return vals, rank


def _count_ranked(rank):
    return jnp.sum((rank < NOT_RANKED).astype(F32), axis=0, keepdims=True)


def _route_one_head(s1, s2, first_only):
    K = PEER_TOPK
    T = s1.shape[1]
    a, rank1 = _extract_top(s1, K, first_only)
    b, rank2 = _extract_top(s2, K, first_only)

    b_all = jnp.concatenate(b, axis=0)
    sub = lax.broadcasted_iota(jnp.int32, (8, T), 0)
    blocks = [a[0] + b_all]
    for p in range(1, K):
        blocks.append(jnp.where(sub < K // (p + 1), a[p] + b_all[:8], -jnp.inf))
    cand = jnp.concatenate(blocks, axis=0)
    best, order = _extract_top(cand, K, first_only)
    chosen = (order < NOT_RANKED).astype(F32)
    counts = [jnp.sum(chosen[:K], axis=0, keepdims=True)]
    for p in range(1, K):
        counts.append(jnp.sum(chosen[K + 8 * (p - 1):K + 8 * p], axis=0, keepdims=True))
    ranked = jnp.maximum(jnp.maximum(_count_ranked(rank1), _count_ranked(rank2)),
                         functools.reduce(jnp.add, counts))

    z = jnp.zeros_like(best[0])
    for k in range(K):
        z = z + jnp.exp(best[k] - best[0])
    c1 = jnp.zeros_like(s1)
    for p in range(K):
        c1 = jnp.where(rank1 == float(p + 1), counts[p], c1)
    return rank2, c1, jnp.exp(s1 - a[0]) / z, jnp.exp(s2 - b[0]), ranked


def _peer_route_kernel(q_ref, keys_ref, r2_ref, c1_ref, e1_ref, e2_ref):
    half = keys_ref.shape[3]

    def route(first_only):
        ranked = []
        for h in range(keys_ref.shape[0]):
            s1 = _dot_nt(keys_ref[h, 0], q_ref[:, 2 * h * half:(2 * h + 1) * half])
            s2 = _dot_nt(keys_ref[h, 1], q_ref[:, (2 * h + 1) * half:(2 * h + 2) * half])
            r2, c1_ref[h], e1_ref[h], e2, n = _route_one_head(s1, s2, first_only)
            r2_ref[h] = r2.astype(r2_ref.dtype)
            e2_ref[h] = e2.astype(e2_ref.dtype)
            ranked.append(n)
        return jnp.max(functools.reduce(jnp.maximum, ranked))

    most_ranked = route(False)

    @pl.when(most_ranked > PEER_TOPK)
    def _():
        route(True)


def _peer_route(q, keys, tt=LANES, heads_per_step=4):
    n = q.shape[0]
    heads, _, nk, half = keys.shape
    hp = heads_per_step
    out = lambda dtype: jax.ShapeDtypeStruct((heads, nk, n), dtype)
    spec = pl.BlockSpec((hp, nk, tt), lambda i, g: (g, 0, i))
    return pl.pallas_call(
        _peer_route_kernel,
        grid=(n // tt, heads // hp),
        in_specs=[pl.BlockSpec((tt, 2 * half * hp), lambda i, g: (i, g)),
                  pl.BlockSpec((hp, 2, nk, half), lambda i, g: (g, 0, 0, 0))],
        out_specs=[spec] * 4,
        out_shape=[out(BF16), out(F32), out(F32), out(BF16)],
        compiler_params=_params("arbitrary", "arbitrary"),
        name="peer_route",
    )(q, keys)


def _peer_expert_kernel(x_ref, u_ref, vta_ref, vtb_ref, r2_ref, e2_ref, c1_ref, e1_ref, o_ref,
                        pa_ref, pb_ref, *, lane_chunk, sub_experts):
    e = pl.program_id(1)
    last = pl.num_programs(1) - 1
    nk = PEER_KEYS
    tb = x_ref.shape[0]
    half = u_ref.shape[0] // 2
    n_sub = half // sub_experts
    d_rows = vta_ref.shape[0] // n_sub

    def rows_to_tile(ref, h, ii, ts):
        row = jnp.broadcast_to(ref[h, ii:ii + 1, ts], (BF16_SUBLANES, lane_chunk)).astype(BF16)
        return jnp.concatenate([row] * (nk // BF16_SUBLANES), axis=0)

    def evaluate(p_ref, first, c):
        rows = slice(first + c * sub_experts, first + (c + 1) * sub_experts)
        act = _gelu(_dot_nt(u_ref[rows, :], x_ref[...])).astype(BF16)
        for r0 in range(0, sub_experts, nk):
            ii = (rows.start + r0) // nk
            for t0 in range(0, tb, lane_chunk):
                ts = slice(t0, t0 + lane_chunk)
                gate = jnp.zeros((nk, lane_chunk), BF16)
                for h in range(PEER_HEADS):
                    picked = r2_ref[h, :, ts] <= rows_to_tile(c1_ref, h, ii, ts)
                    gate = gate + jnp.where(picked, e2_ref[h, :, ts], 0.0) * rows_to_tile(e1_ref, h, ii, ts)
                p_ref[c * sub_experts + r0:c * sub_experts + r0 + nk, ts] = gate * act[r0:r0 + nk, ts]

    def apply(vt_ref, p_ref, c):
        rows = slice(c * d_rows, (c + 1) * d_rows)
        o_ref[rows, :] += _dot(vt_ref[rows, :], p_ref[...])

    @pl.when(e == 0)
    def _():
        o_ref[...] = jnp.zeros_like(o_ref)
        pb_ref[...] = jnp.zeros_like(pb_ref)

    @pl.when(e < last)
    def _():
        for c in range(n_sub):
            apply(vta_ref, pb_ref, c)
            evaluate(pa_ref, 0, c)
        for c in range(n_sub):
            apply(vtb_ref, pa_ref, c)
            evaluate(pb_ref, half, c)

    @pl.when(e == last)
    def _():
        for c in range(n_sub):
            apply(vta_ref, pb_ref, c)


def _peer_experts(x, u, vt, r2, e2, c1, e1, tb=512, eb=1024, lane_chunk=256, sub_experts=256):
    n, d = x.shape
    ne = u.shape[0] // eb
    heads, nk, _ = r2.shape
    tb = min(tb, n)
    rows_i = eb // nk
    half = eb // 2
    tok = lambda t, e: (0, 0, t)
    cur = lambda e: jnp.minimum(e, ne - 1)
    return pl.pallas_call(
        functools.partial(_peer_expert_kernel, lane_chunk=min(lane_chunk, tb), sub_experts=sub_experts),
        grid=(n // tb, ne + 1),
        in_specs=[pl.BlockSpec((tb, d), lambda t, e: (t, 0)),
                  pl.BlockSpec((eb, d), lambda t, e: (cur(e), 0)),
                  pl.BlockSpec((d, half), lambda t, e: (0, jnp.maximum(2 * e - 1, 0))),
                  pl.BlockSpec((d, half), lambda t, e: (0, 2 * cur(e))),
                  pl.BlockSpec((heads, nk, tb), tok),
                  pl.BlockSpec((heads, nk, tb), tok),
                  pl.BlockSpec((heads, rows_i, tb), lambda t, e: (0, cur(e), t)),
                  pl.BlockSpec((heads, rows_i, tb), lambda t, e: (0, cur(e), t))],
        out_specs=pl.BlockSpec((d, tb), lambda t, e: (0, t)),
        out_shape=jax.ShapeDtypeStruct((d, n), F32),
        scratch_shapes=[pltpu.VMEM((half, tb), BF16), pltpu.VMEM((half, tb), BF16)],
        compiler_params=_params("arbitrary", "arbitrary"),
        name="peer_experts",
    )(x, u, vt, vt, r2, e2, c1, e1)


def _residual_ln_t_kernel(h_ref, yt_ref, g_ref, b_ref, o_ref, ob_ref):
    out = _layer_norm(ALPHA * h_ref[...] + yt_ref[...].T, g_ref[...], b_ref[...])
    o_ref[...] = out
    ob_ref[...] = out.astype(BF16)


def _residual_ln_t(h, yt, g, b, tm=512):
    n, d = h.shape
    tm = min(tm, n)
    row = lambda i: (i, 0)
    fixed = lambda i: (0, 0)
    return pl.pallas_call(
        _residual_ln_t_kernel,
        grid=(n // tm,),
        in_specs=[pl.BlockSpec((tm, d), row), pl.BlockSpec((d, tm), lambda i: (0, i)),
                  pl.BlockSpec((1, d), fixed), pl.BlockSpec((1, d), fixed)],
        out_specs=[pl.BlockSpec((tm, d), row), pl.BlockSpec((tm, d), row)],
        out_shape=[jax.ShapeDtypeStruct((n, d), F32), jax.ShapeDtypeStruct((n, d), BF16)],
        compiler_params=_params("arbitrary"),
        name="residual_ln_t",
    )(h, yt, g.reshape(1, d), b.reshape(1, d))


def _peer_layer(h, hb, w_q, sub_keys, u_tab, v_tab, ln_g, ln_b):
    q = _matmul(hb, w_q.astype(BF16), BF16, name="peer_query")
    r2, c1, e1, e2 = _peer_route(q, sub_keys.astype(BF16))
    yt = _peer_experts(hb, u_tab.astype(BF16), v_tab.astype(BF16).T, r2, e2, c1, e1)
    return _residual_ln_t(h, yt, ln_g, ln_b)


def _sb_mixer(hb, batch, seq, w_in, w_out):
    qkv = _matmul(hb, w_in.astype(BF16), BF16, name="sb_in")
    return _sb_attention(qkv, batch, seq), w_out


def _ml_mixer(hb, batch, seq, w_in, b_gates, norm_g, w_out):
    heads = ML_HEADS
    d_gate = w_out.shape[0]
    n_qkv = w_in.shape[1] - d_gate - 2 * heads
    qkv = _matmul(hb, w_in[:, :n_qkv].astype(BF16), BF16, name="ml_in_qkv")
    w_og = jnp.pad(w_in[:, n_qkv:], ((0, 0), (0, LANES - 2 * heads))).astype(BF16)
    og = _matmul(hb, w_og, F32, tm=512, tn=w_og.shape[1], name="ml_in_gates")
    bias = jnp.pad(b_gates.astype(F32), (0, LANES - 2 * heads)).reshape(1, LANES)
    return _mlstm(qkv, og, bias, norm_g, batch, seq), w_out


def _sg_mixer(hb, batch, seq, w_in, norm_g, norm_b, w_s, b_s, w_out):
    uv = _matmul(hb, w_in.astype(BF16), F32, act="gelu", name="sg_in")
    return _sgu(uv, norm_g, norm_b, w_s, b_s), w_out


def kernel(x, l0_sb_w_in, l0_sb_w_out, l0_ln1_g, l0_ln1_b, l0_peer_w_q, l0_peer_sub_keys, l0_peer_u, l0_peer_v, l0_ln2_g, l0_ln2_b, l1_ml_w_in, l1_ml_b_gates, l1_ml_norm_g, l1_ml_w_out, l1_ln1_g, l1_ln1_b, l1_peer_w_q, l1_peer_sub_keys, l1_peer_u, l1_peer_v, l1_ln2_g, l1_ln2_b, l2_sg_w_in, l2_sg_norm_g, l2_sg_norm_b, l2_sg_w_s, l2_sg_b_s, l2_sg_w_out, l2_ln1_g, l2_ln1_b, l2_peer_w_q, l2_peer_sub_keys, l2_peer_u, l2_peer_v, l2_ln2_g, l2_ln2_b, l3_sb_w_in, l3_sb_w_out, l3_ln1_g, l3_ln1_b, l3_peer_w_q, l3_peer_sub_keys, l3_peer_u, l3_peer_v, l3_ln2_g, l3_ln2_b):
    batch, seq, d = x.shape
    mixers = (
        (_sb_mixer, (l0_sb_w_in, l0_sb_w_out)),
        (_ml_mixer, (l1_ml_w_in, l1_ml_b_gates, l1_ml_norm_g, l1_ml_w_out)),
        (_sg_mixer, (l2_sg_w_in, l2_sg_norm_g, l2_sg_norm_b, l2_sg_w_s, l2_sg_b_s, l2_sg_w_out)),
        (_sb_mixer, (l3_sb_w_in, l3_sb_w_out)),
    )
    norm1 = ((l0_ln1_g, l0_ln1_b), (l1_ln1_g, l1_ln1_b), (l2_ln1_g, l2_ln1_b), (l3_ln1_g, l3_ln1_b))
    peers = (
        (l0_peer_w_q, l0_peer_sub_keys, l0_peer_u, l0_peer_v, l0_ln2_g, l0_ln2_b),
        (l1_peer_w_q, l1_peer_sub_keys, l1_peer_u, l1_peer_v, l1_ln2_g, l1_ln2_b),
        (l2_peer_w_q, l2_peer_sub_keys, l2_peer_u, l2_peer_v, l2_ln2_g, l2_ln2_b),
        (l3_peer_w_q, l3_peer_sub_keys, l3_peer_u, l3_peer_v, l3_ln2_g, l3_ln2_b),
    )
    h = x.reshape(batch * seq, d)
    hb = h.astype(BF16)
    for (mixer, mixer_params), (g1, b1), peer_params in zip(mixers, norm1, peers):
        pre, w_out = mixer(hb, batch, seq, *mixer_params)
        h, hb = _proj_residual_ln(pre, w_out.astype(BF16), h, g1, b1)
        h, hb = _peer_layer(h, hb, *peer_params)
    return h.reshape(batch, seq, d)
```

```python
import functools

import jax
import jax.numpy as jnp
from jax import lax
from jax.experimental import pallas as pl
from jax.experimental.pallas import tpu as pltpu

F32 = jnp.float32
BF16 = jnp.bfloat16

LANES = 128
SUBLANES = 8
BF16_SUBLANES = 16
CHUNK = 128
SB_HEADS = 16
ML_HEADS = 8
SG_GROUPS = 8
PEER_HEADS = 8
PEER_KEYS = 128
PEER_TOPK = 16
DEPTH = 4
ALPHA = (2 * DEPTH) ** 0.25
LN_EPS = 1e-5
VMEM_LIMIT_BYTES = 56 * 1024 * 1024
EXP_ZERO_BELOW = -110.0
NOT_RANKED = 99.0


def _params(*sem):
    return pltpu.CompilerParams(dimension_semantics=sem, vmem_limit_bytes=VMEM_LIMIT_BYTES)


def _gelu(x):
    return 0.5 * x * (1.0 + lax.erf(x * 0.7071067811865476))


def _softplus(z):
    return jnp.maximum(z, 0.0) + jnp.log1p(jnp.exp(-jnp.abs(z)))


def _log_sigmoid(z):
    return -_softplus(-z)


def _layer_norm(t, g, b):
    mu = jnp.mean(t, axis=-1, keepdims=True)
    d = t - mu
    var = jnp.mean(d * d, axis=-1, keepdims=True)
    return d * lax.rsqrt(var + LN_EPS) * g + b


def _dot(a, b):
    return jnp.dot(a, b, preferred_element_type=F32)


def _dot_nt(a, b):
    return lax.dot_general(a, b, (((1,), (1,)), ((), ())), preferred_element_type=F32)


def _dot_tn(a, b):
    return lax.dot_general(a, b, (((0,), (0,)), ((), ())), preferred_element_type=F32)


def _mm_kernel(x_ref, w_ref, o_ref, *, act):
    y = _dot(x_ref[...], w_ref[...])
    if act == "gelu":
        y = _gelu(y)
    o_ref[...] = y.astype(o_ref.dtype)


def _matmul(x, w, out_dtype, act=None, tm=1024, tn=1024, name="matmul"):
    m, k = x.shape
    n = w.shape[1]
    tm, tn = min(tm, m), min(tn, n)
    assert m % tm == 0 and n % tn == 0, (m, n, tm, tn)
    return pl.pallas_call(
        functools.partial(_mm_kernel, act=act),
        grid=(n // tn, m // tm),
        in_specs=[pl.BlockSpec((tm, k), lambda j, i: (i, 0)),
                  pl.BlockSpec((k, tn), lambda j, i: (0, j))],
        out_specs=pl.BlockSpec((tm, tn), lambda j, i: (i, j)),
        out_shape=jax.ShapeDtypeStruct((m, n), out_dtype),
        compiler_params=_params("arbitrary", "arbitrary"),
        name=name,
    )(x, w)


def _proj_ln_kernel(x_ref, w_ref, h_ref, g_ref, b_ref, o_ref, ob_ref):
    y = _dot(x_ref[...], w_ref[...])
    out = _layer_norm(ALPHA * h_ref[...] + y, g_ref[...], b_ref[...])
    o_ref[...] = out
    ob_ref[...] = out.astype(BF16)


def _proj_residual_ln(x, w, h, g, b, tm=512):
    m, k = x.shape
    d = w.shape[1]
    tm = min(tm, m)
    assert m % tm == 0
    row = lambda i: (i, 0)
    fixed = lambda i: (0, 0)
    return pl.pallas_call(
        _proj_ln_kernel,
        grid=(m // tm,),
        in_specs=[pl.BlockSpec((tm, k), row), pl.BlockSpec((k, d), fixed),
                  pl.BlockSpec((tm, d), row), pl.BlockSpec((1, d), fixed),
                  pl.BlockSpec((1, d), fixed)],
        out_specs=[pl.BlockSpec((tm, d), row), pl.BlockSpec((tm, d), row)],
        out_shape=[jax.ShapeDtypeStruct((m, d), F32), jax.ShapeDtypeStruct((m, d), BF16)],
        compiler_params=_params("arbitrary"),
        name="proj_residual_ln",
    )(x, w, h, g.reshape(1, d), b.reshape(1, d))


def _sb_kernel(q_ref, k_ref, v_ref, o_ref, *, blk, win, dh, scale):
    qi = pl.program_id(2)
    heads = q_ref.shape[1] // dh
    row = lax.broadcasted_iota(jnp.int32, (win, win), 0)
    col = lax.broadcasted_iota(jnp.int32, (win, win), 1)
    suffix = jnp.concatenate(
        [(row > col).astype(BF16), jnp.ones((win, LANES), BF16)], axis=1)
    q_pos = qi * blk + lax.broadcasted_iota(jnp.int32, (blk, win), 0)
    offset = lax.broadcasted_iota(jnp.int32, (blk, win), 1)

    def cond(c):
        return jnp.logical_and(c[0] > 0, c[1] > EXP_ZERO_BELOW)

    def body(c):
        end, _, carries, accs = c
        start = pl.multiple_of(jnp.maximum(end - win, 0), blk)
        key_pos = start + offset
        valid = jnp.logical_and(key_pos < q_pos, key_pos < end)
        log_beta, log_keep = [], []
        for h in range(heads):
            cols = slice(h * dh, (h + 1) * dh)
            z = _dot_nt(q_ref[:, cols], k_ref[pl.ds(start, win), cols]) * scale
            sp = _softplus(z)
            log_beta.append(z - sp)
            log_keep.append(jnp.where(valid, -sp, 0.0))
        keep = jnp.concatenate(log_keep, axis=0)
        hi = keep.astype(BF16)
        lo = (keep - hi.astype(F32)).astype(BF16)
        sums = _dot(jnp.concatenate([hi, lo], axis=0), suffix)
        sums = sums[:heads * blk] + sums[heads * blk:]
        new_carries, new_accs = [], []
        for h in range(heads):
            cols = slice(h * dh, (h + 1) * dh)
            part = sums[h * blk:(h + 1) * blk]
            between = part[:, :win] + jnp.concatenate([carries[h]] * (win // LANES), axis=1)
            w = jnp.where(valid, jnp.exp(log_beta[h] + between), 0.0)
            new_carries.append(carries[h] + part[:, win:])
            new_accs.append(accs[h] + _dot(w.astype(BF16), v_ref[pl.ds(start, win), cols]))
        largest = jnp.max(functools.reduce(jnp.maximum, new_carries))
        return start, largest, tuple(new_carries), tuple(new_accs)

    init = ((qi + 1) * blk, jnp.float32(0.0), (jnp.zeros((blk, LANES), F32),) * heads,
            (jnp.zeros((blk, dh), F32),) * heads)
    accs = lax.while_loop(cond, body, init)[3]
    for h in range(heads):
        o_ref[:, h * dh:(h + 1) * dh] = accs[h].astype(o_ref.dtype)


def _sb_attention(qkv, batch, seq, blk=CHUNK, win=3 * CHUNK, heads_per_step=8):
    n, three_d = qkv.shape
    d = three_d // 3
    dh = d // SB_HEADS
    groups = SB_HEADS // heads_per_step
    width = heads_per_step * dh
    nq = seq // blk
    win = min(win, seq)
    return pl.pallas_call(
        functools.partial(_sb_kernel, blk=blk, win=win, dh=dh, scale=dh ** -0.5),
        grid=(batch, groups, nq),
        in_specs=[pl.BlockSpec((blk, width), lambda b, g, i: (b * nq + i, g)),
                  pl.BlockSpec((seq, width), lambda b, g, i: (b, groups + g)),
                  pl.BlockSpec((seq, width), lambda b, g, i: (b, 2 * groups + g))],
        out_specs=pl.BlockSpec((blk, width), lambda b, g, i: (b * nq + i, g)),
        out_shape=jax.ShapeDtypeStruct((n, d), BF16),
        compiler_params=_params("arbitrary", "arbitrary", "arbitrary"),
        name="sb_attention",
    )(qkv, qkv, qkv)


def _mlstm_kernel(q_ref, k_ref, v_ref, o_ref, g_ref, bias_ref, ng_ref, y_ref, c_ref, m_ref,
                  *, dk, dv, scale):
    L = q_ref.shape[0]
    heads = ML_HEADS

    @pl.when(pl.program_id(1) == 0)
    def _():
        c_ref[...] = jnp.zeros_like(c_ref)
        m_ref[...] = jnp.zeros_like(m_ref)

    g = g_ref[...] + bias_ref[...]
    log_f = _log_sigmoid(g)
    row = lax.broadcasted_iota(jnp.int32, (L, L), 0)
    col = lax.broadcasted_iota(jnp.int32, (L, L), 1)
    causal = col <= row
    b_cols = jnp.dot(causal.astype(F32), log_f, precision=lax.Precision.HIGHEST,
                     preferred_element_type=F32)
    b_rows = b_cols.T
    g_rows = g.T
    ones_col = (lax.broadcasted_iota(jnp.int32, (L, LANES), 1) == 0).astype(BF16)

    for h in range(heads):
        q = q_ref[:, h * dk:(h + 1) * dk]
        k = k_ref[:, h * dk:(h + 1) * dk]
        v_ext = jnp.concatenate([v_ref[:, h * dv:(h + 1) * dv], ones_col], axis=1)
        li_col = g[:, h:h + 1]
        li_row = g_rows[h:h + 1, :]
        b_col = b_cols[:, heads + h:heads + h + 1]
        b_row = b_rows[heads + h:heads + h + 1, :]
        m_prev = m_ref[h:h + 1, 0:1]
        c_ext = c_ref[h]

        d_intra = jnp.where(causal, b_col - b_row + li_row, -jnp.inf)
        d_inter = b_col + m_prev
        m_t = jnp.maximum(d_inter, jnp.max(d_intra, axis=-1, keepdims=True))
        w_intra = jnp.exp(d_intra - m_t)
        w_inter = jnp.exp(d_inter - m_t)
        s = _dot_nt(q, k) * scale * w_intra
        num = _dot(s.astype(BF16), v_ext) + w_inter * (_dot(q, c_ext.astype(BF16)) * scale)
        den = num[:, dv:dv + 1]
        hid = num[:, :dv] / jnp.maximum(jnp.abs(den), jnp.exp(-m_t))

        b_last = b_col[L - 1:L, :]
        m_new = jnp.maximum(b_last + m_prev,
                            jnp.max(b_last - b_row + li_row, axis=-1, keepdims=True))
        decay = jnp.exp(b_last + m_prev - m_new)
        w_state = jnp.exp(b_last - b_col + li_col - m_new)
        kw = (k.astype(F32) * w_state).astype(BF16)
        c_ref[h] = decay * c_ext + _dot_tn(kw, v_ext)
        m_ref[h:h + 1, :] = jnp.broadcast_to(m_new, (1, LANES))

        hid = hid * lax.rsqrt(jnp.mean(hid * hid, axis=-1, keepdims=True) + LN_EPS)
        hid = hid * ng_ref[:, h * dv:(h + 1) * dv]
        gate = jax.nn.sigmoid(o_ref[:, h * dv:(h + 1) * dv])
        y_ref[:, h * dv:(h + 1) * dv] = (gate * hid).astype(y_ref.dtype)


def _mlstm(qkv, og, bias, norm_g, batch, seq):
    n = qkv.shape[0]
    heads = ML_HEADS
    d = og.shape[1] - LANES
    dv = d // heads
    dk = dv // 2
    L = CHUNK
    nc = seq // L
    tok = lambda b, c: (b * nc + c, 0)
    fixed = lambda b, c: (0, 0)
    return pl.pallas_call(
        functools.partial(_mlstm_kernel, dk=dk, dv=dv, scale=dk ** -0.5),
        grid=(batch, nc),
        in_specs=[pl.BlockSpec((L, heads * dk), tok),
                  pl.BlockSpec((L, heads * dk), lambda b, c: (b * nc + c, 1)),
                  pl.BlockSpec((L, d), lambda b, c: (b * nc + c, 1)),
                  pl.BlockSpec((L, d), tok),
                  pl.BlockSpec((L, LANES), lambda b, c: (b * nc + c, d // LANES)),
                  pl.BlockSpec((1, LANES), fixed),
                  pl.BlockSpec((1, d), fixed)],
        out_specs=pl.BlockSpec((L, d), tok),
        out_shape=jax.ShapeDtypeStruct((n, d), BF16),
        scratch_shapes=[pltpu.VMEM((heads, dk, dv + LANES), F32),
                        pltpu.VMEM((heads, LANES), F32)],
        compiler_params=_params("arbitrary", "arbitrary"),
        name="mlstm",
    )(qkv, qkv, qkv, og, og, bias, norm_g.reshape(1, d))


def _sgu_kernel(u_ref, v_ref, g_ref, b_ref, ws_ref, bs_ref, y_ref):
    L, width = u_ref.shape
    dg = width // SG_GROUPS
    vn = _layer_norm(v_ref[...], g_ref[...], b_ref[...]).astype(BF16)
    row = lax.broadcasted_iota(jnp.int32, (L, L), 0)
    col = lax.broadcasted_iota(jnp.int32, (L, L), 1)
    for g in range(SG_GROUPS):
        ws = jnp.where(col <= row, ws_ref[g], 0.0).astype(BF16)
        mixed = _dot(ws, vn[:, g * dg:(g + 1) * dg]) + bs_ref[:, g:g + 1]
        y_ref[:, g * dg:(g + 1) * dg] = (u_ref[:, g * dg:(g + 1) * dg] * mixed).astype(y_ref.dtype)


def _sgu(uv, norm_g, norm_b, w_s, b_s):
    n, two_w = uv.shape
    width = two_w // 2
    L = CHUNK
    bs_t = jnp.zeros((L, LANES), F32).at[:, :SG_GROUPS].set(b_s.T)
    fixed = lambda i: (0, 0)
    return pl.pallas_call(
        _sgu_kernel,
        grid=(n // L,),
        in_specs=[pl.BlockSpec((L, width), lambda i: (i, 0)),
                  pl.BlockSpec((L, width), lambda i: (i, 1)),
                  pl.BlockSpec((1, width), fixed), pl.BlockSpec((1, width), fixed),
                  pl.BlockSpec((SG_GROUPS, L, L), lambda i: (0, 0, 0)),
                  pl.BlockSpec((L, LANES), fixed)],
        out_specs=pl.BlockSpec((L, width), lambda i: (i, 0)),
        out_shape=jax.ShapeDtypeStruct((n, width), BF16),
        compiler_params=_params("arbitrary"),
        name="sgu",
    )(uv, uv, norm_g.reshape(1, width), norm_b.reshape(1, width), w_s, bs_t)


def _extract_top(x, count, first_only):
    rows = x.shape[0]
    iota = lax.broadcasted_iota(jnp.int32, x.shape, 0)
    rank = jnp.full(x.shape, NOT_RANKED, F32)
    vals = []
    for k in range(count):
        mx = jnp.max(x, axis=0, keepdims=True)
        sel = x == mx
        if first_only:
            sel = iota == jnp.min(jnp.where(sel, iota, rows), axis=0, keepdims=True)
        vals.append(mx)
        rank = jnp.where(sel, float(k + 1), rank)
        x = jnp.where(sel, -jnp.inf, x)
    return vals, rank


def _candidate_layout(top):
    tiles, cur, fill = [], [], 0
    for p in range(top):
        n, q0 = top // (p + 1), 0
        while n > 0:
            if fill == SUBLANES or (n < SUBLANES and fill + n > SUBLANES):
                tiles.append(cur)
                cur, fill = [], 0
            take = min(n, SUBLANES - fill)
            cur.append((p, q0, fill, take))
            fill, q0, n = fill + take, q0 + take, n - take
    tiles.append(cur)
    return [t for t in tiles if t]


def _count_ranked(rank):
    return jnp.sum((rank < NOT_RANKED).astype(F32), axis=0, keepdims=True)


def _route_one_head(s1, s2, first_only):
    K = PEER_TOPK
    T = s1.shape[1]
    a, rank1 = _extract_top(s1, K, first_only)
    b, rank2 = _extract_top(s2, K, first_only)

    b_all = jnp.concatenate(b, axis=0)
    sub = lax.broadcasted_iota(jnp.int32, (SUBLANES, T), 0)
    layout = _candidate_layout(K)
    tiles = []
    for segments in layout:
        tile = jnp.full((SUBLANES, T), -jnp.inf, F32)
        for p, q0, off, n in segments:
            src = b_all[q0:q0 + SUBLANES]
            if off:
                src = pltpu.roll(src, off, 0)
            rows = jnp.logical_and(sub >= off, sub < off + n)
            tile = jnp.where(rows, a[p] + src, tile)
        tiles.append(tile)
    cand = jnp.concatenate(tiles, axis=0)
    best, order = _extract_top(cand, K, first_only)
    chosen = (order < NOT_RANKED).astype(F32)
    counts = [jnp.zeros((1, T), F32) for _ in range(K)]
    for v, segments in enumerate(layout):
        part = chosen[v * SUBLANES:(v + 1) * SUBLANES]
        for p, _, off, n in segments:
            rows = jnp.logical_and(sub >= off, sub < off + n)
            counts[p] = counts[p] + jnp.sum(jnp.where(rows, part, 0.0), axis=0, keepdims=True)
    ranked = jnp.maximum(jnp.maximum(_count_ranked(rank1), _count_ranked(rank2)),
                         functools.reduce(jnp.add, counts))

    z = jnp.zeros_like(best[0])
    for k in range(K):
        z = z + jnp.exp(best[k] - best[0])
    c1 = jnp.zeros_like(s1)
    for p in range(K):
        c1 = jnp.where(rank1 == float(p + 1), counts[p], c1)
    return rank2, c1, jnp.exp(s1 - a[0]) / z, jnp.exp(s2 - b[0]), ranked


def _peer_route_kernel(q_ref, keys_ref, r2_ref, c1_ref, e1_ref, e2_ref):
    half = keys_ref.shape[3]

    def route(first_only):
        ranked = []
        for h in range(keys_ref.shape[0]):
            s1 = _dot_nt(keys_ref[h, 0], q_ref[:, 2 * h * half:(2 * h + 1) * half])
            s2 = _dot_nt(keys_ref[h, 1], q_ref[:, (2 * h + 1) * half:(2 * h + 2) * half])
            r2, c1_ref[h], e1_ref[h], e2, n = _route_one_head(s1, s2, first_only)
            r2_ref[h] = r2.astype(r2_ref.dtype)
            e2_ref[h] = e2.astype(e2_ref.dtype)
            ranked.append(n)
        return jnp.max(functools.reduce(jnp.maximum, ranked))

    most_ranked = route(False)

    @pl.when(most_ranked > PEER_TOPK)
    def _():
        route(True)


def _peer_route(q, keys, tt=LANES, heads_per_step=4):
    n = q.shape[0]
    heads, _, nk, half = keys.shape
    hp = heads_per_step
    out = lambda dtype: jax.ShapeDtypeStruct((heads, nk, n), dtype)
    spec = pl.BlockSpec((hp, nk, tt), lambda i, g: (g, 0, i))
    return pl.pallas_call(
        _peer_route_kernel,
        grid=(n // tt, heads // hp),
        in_specs=[pl.BlockSpec((tt, 2 * half * hp), lambda i, g: (i, g)),
                  pl.BlockSpec((hp, 2, nk, half), lambda i, g: (g, 0, 0, 0))],
        out_specs=[spec] * 4,
        out_shape=[out(BF16), out(F32), out(F32), out(BF16)],
        compiler_params=_params("arbitrary", "arbitrary"),
        name="peer_route",
    )(q, keys)


def _peer_expert_kernel(x_ref, u_ref, vta_ref, vtb_ref, r2_ref, e2_ref, c1_ref, e1_ref, o_ref,
                        pa_ref, pb_ref, *, lane_chunk, sub_experts):
    e = pl.program_id(1)
    last = pl.num_programs(1) - 1
    nk = PEER_KEYS
    tb = x_ref.shape[0]
    half = u_ref.shape[0] // 2
    n_sub = half // sub_experts
    d_rows = vta_ref.shape[0] // n_sub

    def rows_to_tile(ref, h, ii, ts):
        row = jnp.broadcast_to(ref[h, ii:ii + 1, ts], (BF16_SUBLANES, lane_chunk)).astype(BF16)
        return jnp.concatenate([row] * (nk // BF16_SUBLANES), axis=0)

    def evaluate(p_ref, first, c):
        rows = slice(first + c * sub_experts, first + (c + 1) * sub_experts)
        act = _gelu(_dot_nt(u_ref[rows, :], x_ref[...])).astype(BF16)
        for r0 in range(0, sub_experts, nk):
            ii = (rows.start + r0) // nk
            for t0 in range(0, tb, lane_chunk):
                ts = slice(t0, t0 + lane_chunk)
                gate = jnp.zeros((nk, lane_chunk), BF16)
                for h in range(PEER_HEADS):
                    picked = r2_ref[h, :, ts] <= rows_to_tile(c1_ref, h, ii, ts)
                    gate = gate + jnp.where(picked, e2_ref[h, :, ts], 0.0) * rows_to_tile(e1_ref, h, ii, ts)
                p_ref[c * sub_experts + r0:c * sub_experts + r0 + nk, ts] = gate * act[r0:r0 + nk, ts]

    def apply(vt_ref, p_ref, c):
        rows = slice(c * d_rows, (c + 1) * d_rows)
        o_ref[rows, :] += _dot(vt_ref[rows, :], p_ref[...])

    @pl.when(e == 0)
    def _():
        o_ref[...] = jnp.zeros_like(o_ref)
        pb_ref[...] = jnp.zeros_like(pb_ref)

    @pl.when(e < last)
    def _():
        for c in range(n_sub):
            apply(vta_ref, pb_ref, c)
            evaluate(pa_ref, 0, c)
        for c in range(n_sub):
            apply(vtb_ref, pa_ref, c)
            evaluate(pb_ref, half, c)

    @pl.when(e == last)
    def _():
        for c in range(n_sub):
            apply(vta_ref, pb_ref, c)


def _peer_experts(x, u, vt, r2, e2, c1, e1, tb=512, eb=1024, lane_chunk=256, sub_experts=256):
    n, d = x.shape
    ne = u.shape[0] // eb
    heads, nk, _ = r2.shape
    tb = min(tb, n)
    rows_i = eb // nk
    half = eb // 2
    tok = lambda t, e: (0, 0, t)
    cur = lambda e: jnp.minimum(e, ne - 1)
    return pl.pallas_call(
        functools.partial(_peer_expert_kernel, lane_chunk=min(lane_chunk, tb), sub_experts=sub_experts),
        grid=(n // tb, ne + 1),
        in_specs=[pl.BlockSpec((tb, d), lambda t, e: (t, 0)),
                  pl.BlockSpec((eb, d), lambda t, e: (cur(e), 0)),
                  pl.BlockSpec((d, half), lambda t, e: (0, jnp.maximum(2 * e - 1, 0))),
                  pl.BlockSpec((d, half), lambda t, e: (0, 2 * cur(e))),
                  pl.BlockSpec((heads, nk, tb), tok),
                  pl.BlockSpec((heads, nk, tb), tok),
                  pl.BlockSpec((heads, rows_i, tb), lambda t, e: (0, cur(e), t)),
                  pl.BlockSpec((heads, rows_i, tb), lambda t, e: (0, cur(e), t))],
        out_specs=pl.BlockSpec((d, tb), lambda t, e: (0, t)),
        out_shape=jax.ShapeDtypeStruct((d, n), F32),
        scratch_shapes=[pltpu.VMEM((half, tb), BF16), pltpu.VMEM((half, tb), BF16)],
        compiler_params=_params("arbitrary", "arbitrary"),
        name="peer_experts",
    )(x, u, vt, vt, r2, e2, c1, e1)


def _residual_ln_t_kernel(h_ref, yt_ref, g_ref, b_ref, o_ref, ob_ref):
    out = _layer_norm(ALPHA * h_ref[...] + yt_ref[...].T, g_ref[...], b_ref[...])
    o_ref[...] = out
    ob_ref[...] = out.astype(BF16)


def _residual_ln_t(h, yt, g, b, tm=512):
    n, d = h.shape
    tm = min(tm, n)
    row = lambda i: (i, 0)
    fixed = lambda i: (0, 0)
    return pl.pallas_call(
        _residual_ln_t_kernel,
        grid=(n // tm,),
        in_specs=[pl.BlockSpec((tm, d), row), pl.BlockSpec((d, tm), lambda i: (0, i)),
                  pl.BlockSpec((1, d), fixed), pl.BlockSpec((1, d), fixed)],
        out_specs=[pl.BlockSpec((tm, d), row), pl.BlockSpec((tm, d), row)],
        out_shape=[jax.ShapeDtypeStruct((n, d), F32), jax.ShapeDtypeStruct((n, d), BF16)],
        compiler_params=_params("arbitrary"),
        name="residual_ln_t",
    )(h, yt, g.reshape(1, d), b.reshape(1, d))


def _peer_layer(h, hb, w_q, sub_keys, u_tab, v_tab, ln_g, ln_b):
    q = _matmul(hb, w_q.astype(BF16), BF16, name="peer_query")
    r2, c1, e1, e2 = _peer_route(q, sub_keys.astype(BF16))
    yt = _peer_experts(hb, u_tab.astype(BF16), v_tab.astype(BF16).T, r2, e2, c1, e1)
    return _residual_ln_t(h, yt, ln_g, ln_b)


def _sb_mixer(hb, batch, seq, w_in, w_out):
    qkv = _matmul(hb, w_in.astype(BF16), BF16, name="sb_in")
    return _sb_attention(qkv, batch, seq), w_out


def _ml_mixer(hb, batch, seq, w_in, b_gates, norm_g, w_out):
    heads = ML_HEADS
    d_gate = w_out.shape[0]
    n_qkv = w_in.shape[1] - d_gate - 2 * heads
    qkv = _matmul(hb, w_in[:, :n_qkv].astype(BF16), BF16, name="ml_in_qkv")
    w_og = jnp.pad(w_in[:, n_qkv:], ((0, 0), (0, LANES - 2 * heads))).astype(BF16)
    og = _matmul(hb, w_og, F32, tm=512, tn=w_og.shape[1], name="ml_in_gates")
    bias = jnp.pad(b_gates.astype(F32), (0, LANES - 2 * heads)).reshape(1, LANES)
    return _mlstm(qkv, og, bias, norm_g, batch, seq), w_out


def _sg_mixer(hb, batch, seq, w_in, norm_g, norm_b, w_s, b_s, w_out):
    uv = _matmul(hb, w_in.astype(BF16), F32, act="gelu", name="sg_in")
    return _sgu(uv, norm_g, norm_b, w_s, b_s), w_out


def kernel(x, l0_sb_w_in, l0_sb_w_out, l0_ln1_g, l0_ln1_b, l0_peer_w_q, l0_peer_sub_keys, l0_peer_u, l0_peer_v, l0_ln2_g, l0_ln2_b, l1_ml_w_in, l1_ml_b_gates, l1_ml_norm_g, l1_ml_w_out, l1_ln1_g, l1_ln1_b, l1_peer_w_q, l1_peer_sub_keys, l1_peer_u, l1_peer_v, l1_ln2_g, l1_ln2_b, l2_sg_w_in, l2_sg_norm_g, l2_sg_norm_b, l2_sg_w_s, l2_sg_b_s, l2_sg_w_out, l2_ln1_g, l2_ln1_b, l2_peer_w_q, l2_peer_sub_keys, l2_peer_u, l2_peer_v, l2_ln2_g, l2_ln2_b, l3_sb_w_in, l3_sb_w_out, l3_ln1_g, l3_ln1_b, l3_peer_w_q, l3_peer_sub_keys, l3_peer_u, l3_peer_v, l3_ln2_g, l3_ln2_b):
    batch, seq, d = x.shape
    mixers = (
        (_sb_mixer, (l0_sb_w_in, l0_sb_w_out)),
        (_ml_mixer, (l1_ml_w_in, l1_ml_b_gates, l1_ml_norm_g, l1_ml_w_out)),
        (_sg_mixer, (l2_sg_w_in, l2_sg_norm_g, l2_sg_norm_b, l2_sg_w_s, l2_sg_b_s, l2_sg_w_out)),
        (_sb_mixer, (l3_sb_w_in, l3_sb_w_out)),
    )
    norm1 = ((l0_ln1_g, l0_ln1_b), (l1_ln1_g, l1_ln1_b), (l2_ln1_g, l2_ln1_b), (l3_ln1_g, l3_ln1_b))
    peers = (
        (l0_peer_w_q, l0_peer_sub_keys, l0_peer_u, l0_peer_v, l0_ln2_g, l0_ln2_b),
        (l1_peer_w_q, l1_peer_sub_keys, l1_peer_u, l1_peer_v, l1_ln2_g, l1_ln2_b),
        (l2_peer_w_q, l2_peer_sub_keys, l2_peer_u, l2_peer_v, l2_ln2_g, l2_ln2_b),
        (l3_peer_w_q, l3_peer_sub_keys, l3_peer_u, l3_peer_v, l3_ln2_g, l3_ln2_b),
    )
    h = x.reshape(batch * seq, d)
    hb = h.astype(BF16)
    for (mixer, mixer_params), (g1, b1), peer_params in zip(mixers, norm1, peers):
        pre, w_out = mixer(hb, batch, seq, *mixer_params)
        h, hb = _proj_residual_ln(pre, w_out.astype(BF16), h, g1, b1)
        h, hb = _peer_layer(h, hb, *peer_params)
    return h.reshape(batch, seq, d)
```

```python
import functools

import jax
import jax.numpy as jnp
from jax import lax
from jax.experimental import pallas as pl
from jax.experimental.pallas import tpu as pltpu

F32 = jnp.float32
BF16 = jnp.bfloat16

LANES = 128
SUBLANES = 8
BF16_SUBLANES = 16
CHUNK = 128
SB_HEADS = 16
ML_HEADS = 8
SG_GROUPS = 8
PEER_HEADS = 8
PEER_KEYS = 128
PEER_TOPK = 16
DEPTH = 4
ALPHA = (2 * DEPTH) ** 0.25
LN_EPS = 1e-5
VMEM_LIMIT_BYTES = 56 * 1024 * 1024
EXP_ZERO_BELOW = -110.0
NOT_RANKED = 99.0


def _params(*sem):
    return pltpu.CompilerParams(dimension_semantics=sem, vmem_limit_bytes=VMEM_LIMIT_BYTES)


def _gelu(x):
    return 0.5 * x * (1.0 + lax.erf(x * 0.7071067811865476))


def _softplus(z):
    return jnp.maximum(z, 0.0) + jnp.log1p(jnp.exp(-jnp.abs(z)))


def _log_sigmoid(z):
    return -_softplus(-z)


def _layer_norm(t, g, b):
    mu = jnp.mean(t, axis=-1, keepdims=True)
    d = t - mu
    var = jnp.mean(d * d, axis=-1, keepdims=True)
    return d * lax.rsqrt(var + LN_EPS) * g + b


def _dot(a, b):
    return jnp.dot(a, b, preferred_element_type=F32)


def _dot_nt(a, b):
    return lax.dot_general(a, b, (((1,), (1,)), ((), ())), preferred_element_type=F32)


def _dot_tn(a, b):
    return lax.dot_general(a, b, (((0,), (0,)), ((), ())), preferred_element_type=F32)


def _mm_kernel(x_ref, w_ref, o_ref, *, act):
    y = _dot(x_ref[...], w_ref[...])
    if act == "gelu":
        y = _gelu(y)
    o_ref[...] = y.astype(o_ref.dtype)


def _matmul(x, w, out_dtype, act=None, tm=1024, tn=1024, name="matmul"):
    m, k = x.shape
    n = w.shape[1]
    tm, tn = min(tm, m), min(tn, n)
    assert m % tm == 0 and n % tn == 0, (m, n, tm, tn)
    return pl.pallas_call(
        functools.partial(_mm_kernel, act=act),
        grid=(n // tn, m // tm),
        in_specs=[pl.BlockSpec((tm, k), lambda j, i: (i, 0)),
                  pl.BlockSpec((k, tn), lambda j, i: (0, j))],
        out_specs=pl.BlockSpec((tm, tn), lambda j, i: (i, j)),
        out_shape=jax.ShapeDtypeStruct((m, n), out_dtype),
        compiler_params=_params("arbitrary", "arbitrary"),
        name=name,
    )(x, w)


def _proj_ln_kernel(x_ref, w_ref, h_ref, g_ref, b_ref, o_ref, ob_ref):
    y = _dot(x_ref[...], w_ref[...])
    out = _layer_norm(ALPHA * h_ref[...] + y, g_ref[...], b_ref[...])
    o_ref[...] = out
    ob_ref[...] = out.astype(BF16)


def _proj_residual_ln(x, w, h, g, b, tm=512):
    m, k = x.shape
    d = w.shape[1]
    tm = min(tm, m)
    assert m % tm == 0
    row = lambda i: (i, 0)
    fixed = lambda i: (0, 0)
    return pl.pallas_call(
        _proj_ln_kernel,
        grid=(m // tm,),
        in_specs=[pl.BlockSpec((tm, k), row), pl.BlockSpec((k, d), fixed),
                  pl.BlockSpec((tm, d), row), pl.BlockSpec((1, d), fixed),
                  pl.BlockSpec((1, d), fixed)],
        out_specs=[pl.BlockSpec((tm, d), row), pl.BlockSpec((tm, d), row)],
        out_shape=[jax.ShapeDtypeStruct((m, d), F32), jax.ShapeDtypeStruct((m, d), BF16)],
        compiler_params=_params("arbitrary"),
        name="proj_residual_ln",
    )(x, w, h, g.reshape(1, d), b.reshape(1, d))


def _sb_kernel(q_ref, k_ref, v_ref, o_ref, *, blk, win, dh, scale):
    qi = pl.program_id(2)
    heads = q_ref.shape[1] // dh
    row = lax.broadcasted_iota(jnp.int32, (win, win), 0)
    col = lax.broadcasted_iota(jnp.int32, (win, win), 1)
    suffix = jnp.concatenate(
        [(row > col).astype(BF16), jnp.ones((win, LANES), BF16)], axis=1)
    q_pos = qi * blk + lax.broadcasted_iota(jnp.int32, (blk, win), 0)
    offset = lax.broadcasted_iota(jnp.int32, (blk, win), 1)

    def cond(c):
        return jnp.logical_and(c[0] > 0, c[1] > EXP_ZERO_BELOW)

    def body(c):
        end, _, carries, accs = c
        start = pl.multiple_of(jnp.maximum(end - win, 0), blk)
        key_pos = start + offset
        valid = jnp.logical_and(key_pos < q_pos, key_pos < end)
        log_beta, log_keep = [], []
        for h in range(heads):
            cols = slice(h * dh, (h + 1) * dh)
            z = _dot_nt(q_ref[:, cols], k_ref[pl.ds(start, win), cols]) * scale
            sp = _softplus(z)
            log_beta.append(z - sp)
            log_keep.append(jnp.where(valid, -sp, 0.0))
        keep = jnp.concatenate(log_keep, axis=0)
        hi = keep.astype(BF16)
        lo = (keep - hi.astype(F32)).astype(BF16)
        sums = _dot(jnp.concatenate([hi, lo], axis=0), suffix)
        sums = sums[:heads * blk] + sums[heads * blk:]
        new_carries, new_accs = [], []
        for h in range(heads):
            cols = slice(h * dh, (h + 1) * dh)
            part = sums[h * blk:(h + 1) * blk]
            between = part[:, :win] + jnp.concatenate([carries[h]] * (win // LANES), axis=1)
            w = jnp.where(valid, jnp.exp(log_beta[h] + between), 0.0)
            new_carries.append(carries[h] + part[:, win:])
            new_accs.append(accs[h] + _dot(w.astype(BF16), v_ref[pl.ds(start, win), cols]))
        largest = jnp.max(functools.reduce(jnp.maximum, new_carries))
        return start, largest, tuple(new_carries), tuple(new_accs)

    init = ((qi + 1) * blk, jnp.float32(0.0), (jnp.zeros((blk, LANES), F32),) * heads,
            (jnp.zeros((blk, dh), F32),) * heads)
    accs = lax.while_loop(cond, body, init)[3]
    for h in range(heads):
        o_ref[:, h * dh:(h + 1) * dh] = accs[h].astype(o_ref.dtype)


def _sb_attention(qkv, batch, seq, blk=CHUNK, win=3 * CHUNK, heads_per_step=8):
    n, three_d = qkv.shape
    d = three_d // 3
    dh = d // SB_HEADS
    groups = SB_HEADS // heads_per_step
    width = heads_per_step * dh
    nq = seq // blk
    win = min(win, seq)
    return pl.pallas_call(
        functools.partial(_sb_kernel, blk=blk, win=win, dh=dh, scale=dh ** -0.5),
        grid=(batch, groups, nq),
        in_specs=[pl.BlockSpec((blk, width), lambda b, g, i: (b * nq + i, g)),
                  pl.BlockSpec((seq, width), lambda b, g, i: (b, groups + g)),
                  pl.BlockSpec((seq, width), lambda b, g, i: (b, 2 * groups + g))],
        out_specs=pl.BlockSpec((blk, width), lambda b, g, i: (b * nq + i, g)),
        out_shape=jax.ShapeDtypeStruct((n, d), BF16),
        compiler_params=_params("arbitrary", "arbitrary", "arbitrary"),
        name="sb_attention",
    )(qkv, qkv, qkv)


def _mlstm_kernel(q_ref, k_ref, v_ref, o_ref, g_ref, bias_ref, ng_ref, y_ref, c_ref, m_ref,
                  *, dk, dv, scale):
    L = q_ref.shape[0]
    heads = ML_HEADS

    @pl.when(pl.program_id(1) == 0)
    def _():
        c_ref[...] = jnp.zeros_like(c_ref)
        m_ref[...] = jnp.zeros_like(m_ref)

    g = g_ref[...] + bias_ref[...]
    log_f = _log_sigmoid(g)
    row = lax.broadcasted_iota(jnp.int32, (L, L), 0)
    col = lax.broadcasted_iota(jnp.int32, (L, L), 1)
    causal = col <= row
    b_cols = jnp.dot(causal.astype(F32), log_f, precision=lax.Precision.HIGHEST,
                     preferred_element_type=F32)
    b_rows = b_cols.T
    g_rows = g.T
    ones_col = (lax.broadcasted_iota(jnp.int32, (L, LANES), 1) == 0).astype(BF16)

    for h in range(heads):
        q = q_ref[:, h * dk:(h + 1) * dk]
        k = k_ref[:, h * dk:(h + 1) * dk]
        v_ext = jnp.concatenate([v_ref[:, h * dv:(h + 1) * dv], ones_col], axis=1)
        li_col = g[:, h:h + 1]
        li_row = g_rows[h:h + 1, :]
        b_col = b_cols[:, heads + h:heads + h + 1]
        b_row = b_rows[heads + h:heads + h + 1, :]
        m_prev = m_ref[h:h + 1, 0:1]
        c_ext = c_ref[h]

        d_intra = jnp.where(causal, b_col - b_row + li_row, -jnp.inf)
        d_inter = b_col + m_prev
        m_t = jnp.maximum(d_inter, jnp.max(d_intra, axis=-1, keepdims=True))
        w_intra = jnp.exp(d_intra - m_t)
        w_inter = jnp.exp(d_inter - m_t)
        s = _dot_nt(q, k) * scale * w_intra
        num = _dot(s.astype(BF16), v_ext) + w_inter * (_dot(q, c_ext.astype(BF16)) * scale)
        den = num[:, dv:dv + 1]
        hid = num[:, :dv] / jnp.maximum(jnp.abs(den), jnp.exp(-m_t))

        b_last = b_col[L - 1:L, :]
        m_new = jnp.maximum(b_last + m_prev,
                            jnp.max(b_last - b_row + li_row, axis=-1, keepdims=True))
        decay = jnp.exp(b_last + m_prev - m_new)
        w_state = jnp.exp(b_last - b_col + li_col - m_new)
        kw = (k.astype(F32) * w_state).astype(BF16)
        c_ref[h] = decay * c_ext + _dot_tn(kw, v_ext)
        m_ref[h:h + 1, :] = jnp.broadcast_to(m_new, (1, LANES))

        hid = hid * lax.rsqrt(jnp.mean(hid * hid, axis=-1, keepdims=True) + LN_EPS)
        hid = hid * ng_ref[:, h * dv:(h + 1) * dv]
        gate = jax.nn.sigmoid(o_ref[:, h * dv:(h + 1) * dv])
        y_ref[:, h * dv:(h + 1) * dv] = (gate * hid).astype(y_ref.dtype)


def _mlstm(qkv, og, bias, norm_g, batch, seq):
    n = qkv.shape[0]
    heads = ML_HEADS
    d = og.shape[1] - LANES
    dv = d // heads
    dk = dv // 2
    L = CHUNK
    nc = seq // L
    tok = lambda b, c: (b * nc + c, 0)
    fixed = lambda b, c: (0, 0)
    return pl.pallas_call(
        functools.partial(_mlstm_kernel, dk=dk, dv=dv, scale=dk ** -0.5),
        grid=(batch, nc),
        in_specs=[pl.BlockSpec((L, heads * dk), tok),
                  pl.BlockSpec((L, heads * dk), lambda b, c: (b * nc + c, 1)),
                  pl.BlockSpec((L, d), lambda b, c: (b * nc + c, 1)),
                  pl.BlockSpec((L, d), tok),
                  pl.BlockSpec((L, LANES), lambda b, c: (b * nc + c, d // LANES)),
                  pl.BlockSpec((1, LANES), fixed),
                  pl.BlockSpec((1, d), fixed)],
        out_specs=pl.BlockSpec((L, d), tok),
        out_shape=jax.ShapeDtypeStruct((n, d), BF16),
        scratch_shapes=[pltpu.VMEM((heads, dk, dv + LANES), F32),
                        pltpu.VMEM((heads, LANES), F32)],
        compiler_params=_params("arbitrary", "arbitrary"),
        name="mlstm",
    )(qkv, qkv, qkv, og, og, bias, norm_g.reshape(1, d))


def _sgu_kernel(u_ref, v_ref, g_ref, b_ref, ws_ref, bs_ref, y_ref):
    L, width = u_ref.shape
    dg = width // SG_GROUPS
    vn = _layer_norm(v_ref[...], g_ref[...], b_ref[...]).astype(BF16)
    row = lax.broadcasted_iota(jnp.int32, (L, L), 0)
    col = lax.broadcasted_iota(jnp.int32, (L, L), 1)
    for g in range(SG_GROUPS):
        ws = jnp.where(col <= row, ws_ref[g], 0.0).astype(BF16)
        mixed = _dot(ws, vn[:, g * dg:(g + 1) * dg]) + bs_ref[:, g:g + 1]
        y_ref[:, g * dg:(g + 1) * dg] = (u_ref[:, g * dg:(g + 1) * dg] * mixed).astype(y_ref.dtype)


def _sgu(uv, norm_g, norm_b, w_s, b_s):
    n, two_w = uv.shape
    width = two_w // 2
    L = CHUNK
    bs_t = jnp.zeros((L, LANES), F32).at[:, :SG_GROUPS].set(b_s.T)
    fixed = lambda i: (0, 0)
    return pl.pallas_call(
        _sgu_kernel,
        grid=(n // L,),
        in_specs=[pl.BlockSpec((L, width), lambda i: (i, 0)),
                  pl.BlockSpec((L, width), lambda i: (i, 1)),
                  pl.BlockSpec((1, width), fixed), pl.BlockSpec((1, width), fixed),
                  pl.BlockSpec((SG_GROUPS, L, L), lambda i: (0, 0, 0)),
                  pl.BlockSpec((L, LANES), fixed)],
        out_specs=pl.BlockSpec((L, width), lambda i: (i, 0)),
        out_shape=jax.ShapeDtypeStruct((n, width), BF16),
        compiler_params=_params("arbitrary"),
        name="sgu",
    )(uv, uv, norm_g.reshape(1, width), norm_b.reshape(1, width), w_s, bs_t)


def _extract_top(x, count, first_only):
    rows = x.shape[0]
    iota = lax.broadcasted_iota(jnp.int32, x.shape, 0)
    rank = jnp.full(x.shape, NOT_RANKED, F32)
    vals = []
    for k in range(count):
        mx = jnp.max(x, axis=0, keepdims=True)
        sel = x == mx
        if first_only:
            sel = iota == jnp.min(jnp.where(sel, iota, rows), axis=0, keepdims=True)
        vals.append(mx)
        rank = jnp.where(sel, float(k + 1), rank)
        x = jnp.where(sel, -jnp.inf, x)
    return vals, rank


def _candidate_layout(top):
    tiles, cur, fill = [], [], 0
    for p in range(top):
        n, q0 = top // (p + 1), 0
        while n > 0:
            if fill == SUBLANES or (n < SUBLANES and fill + n > SUBLANES):
                tiles.append(cur)
                cur, fill = [], 0
            take = min(n, SUBLANES - fill)
            cur.append((p, q0, fill, take))
            fill, q0, n = fill + take, q0 + take, n - take
    tiles.append(cur)
    return [t for t in tiles if t]


def _count_ranked(rank):
    return jnp.sum((rank < NOT_RANKED).astype(F32), axis=0, keepdims=True)


def _route_one_head(s1, s2, first_only):
    K = PEER_TOPK
    T = s1.shape[1]
    a, rank1 = _extract_top(s1, K, first_only)
    b, rank2 = _extract_top(s2, K, first_only)

    b_all = jnp.concatenate(b, axis=0)
    sub = lax.broadcasted_iota(jnp.int32, (SUBLANES, T), 0)
    layout = _candidate_layout(K)
    tiles = []
    for segments in layout:
        tile = jnp.full((SUBLANES, T), -jnp.inf, F32)
        for p, q0, off, n in segments:
            src = b_all[q0:q0 + SUBLANES]
            if off:
                src = pltpu.roll(src, off, 0)
            rows = jnp.logical_and(sub >= off, sub < off + n)
            tile = jnp.where(rows, a[p] + src, tile)
        tiles.append(tile)
    cand = jnp.concatenate(tiles, axis=0)
    best, order = _extract_top(cand, K, first_only)
    chosen = (order < NOT_RANKED).astype(F32)
    counts = [jnp.zeros((1, T), F32) for _ in range(K)]
    for v, segments in enumerate(layout):
        part = chosen[v * SUBLANES:(v + 1) * SUBLANES]
        for p, _, off, n in segments:
            rows = jnp.logical_and(sub >= off, sub < off + n)
            counts[p] = counts[p] + jnp.sum(jnp.where(rows, part, 0.0), axis=0, keepdims=True)
    ranked = jnp.maximum(jnp.maximum(_count_ranked(rank1), _count_ranked(rank2)),
                         functools.reduce(jnp.add, counts))

    z = jnp.zeros_like(best[0])
    for k in range(K):
        z = z + jnp.exp(best[k] - best[0])
    c1 = jnp.zeros_like(s1)
    for p in range(K):
        c1 = jnp.where(rank1 == float(p + 1), counts[p], c1)
    return rank2, c1, jnp.exp(s1 - a[0]) / z, jnp.exp(s2 - b[0]), ranked


def _peer_route_kernel(q_ref, keys_ref, r2_ref, c1_ref, e1_ref, e2_ref):
    half = keys_ref.shape[3]

    def route(first_only):
        ranked = []
        for h in range(keys_ref.shape[0]):
            s1 = _dot_nt(keys_ref[h, 0], q_ref[:, 2 * h * half:(2 * h + 1) * half])
            s2 = _dot_nt(keys_ref[h, 1], q_ref[:, (2 * h + 1) * half:(2 * h + 2) * half])
            r2, c1_ref[h], e1_ref[h], e2, n = _route_one_head(s1, s2, first_only)
            r2_ref[h] = r2.astype(r2_ref.dtype)
            e2_ref[h] = e2.astype(e2_ref.dtype)
            ranked.append(n)
        return jnp.max(functools.reduce(jnp.maximum, ranked))

    most_ranked = route(False)

    @pl.when(most_ranked > PEER_TOPK)
    def _():
        route(True)


def _peer_route(q, keys, tt=LANES, heads_per_step=4):
    n = q.shape[0]
    heads, _, nk, half = keys.shape
    hp = heads_per_step
    out = lambda dtype: jax.ShapeDtypeStruct((heads, nk, n), dtype)
    spec = pl.BlockSpec((hp, nk, tt), lambda i, g: (g, 0, i))
    return pl.pallas_call(
        _peer_route_kernel,
        grid=(n // tt, heads // hp),
        in_specs=[pl.BlockSpec((tt, 2 * half * hp), lambda i, g: (i, g)),
                  pl.BlockSpec((hp, 2, nk, half), lambda i, g: (g, 0, 0, 0))],
        out_specs=[spec] * 4,
        out_shape=[out(BF16), out(F32), out(F32), out(BF16)],
        compiler_params=_params("arbitrary", "arbitrary"),
        name="peer_route",
    )(q, keys)


def _peer_expert_kernel(xt_ref, u_ref, vta_ref, vtb_ref, r2_ref, e2_ref, c1_ref, e1_ref, o_ref,
                        pa_ref, pb_ref, *, lane_chunk, sub_experts):
    e = pl.program_id(1)
    last = pl.num_programs(1) - 1
    nk = PEER_KEYS
    tb = xt_ref.shape[1]
    half = u_ref.shape[0] // 2
    n_sub = half // sub_experts
    d_rows = vta_ref.shape[0] // n_sub

    def rows_to_tile(ref, h, ii, ts):
        row = jnp.broadcast_to(ref[h, ii:ii + 1, ts], (BF16_SUBLANES, lane_chunk)).astype(BF16)
        return jnp.concatenate([row] * (nk // BF16_SUBLANES), axis=0)

    def evaluate(p_ref, first, c):
        rows = slice(first + c * sub_experts, first + (c + 1) * sub_experts)
        act = _gelu(_dot(u_ref[rows, :], xt_ref[...])).astype(BF16)
        for r0 in range(0, sub_experts, nk):
            ii = (rows.start + r0) // nk
            for t0 in range(0, tb, lane_chunk):
                ts = slice(t0, t0 + lane_chunk)
                gate = jnp.zeros((nk, lane_chunk), BF16)
                for h in range(PEER_HEADS):
                    picked = r2_ref[h, :, ts] <= rows_to_tile(c1_ref, h, ii, ts)
                    gate = gate + jnp.where(picked, e2_ref[h, :, ts], 0.0) * rows_to_tile(e1_ref, h, ii, ts)
                p_ref[c * sub_experts + r0:c * sub_experts + r0 + nk, ts] = gate * act[r0:r0 + nk, ts]

    def apply(vt_ref, p_ref, c):
        rows = slice(c * d_rows, (c + 1) * d_rows)
        o_ref[rows, :] += _dot(vt_ref[rows, :], p_ref[...])

    @pl.when(e == 0)
    def _():
        o_ref[...] = jnp.zeros_like(o_ref)
        pb_ref[...] = jnp.zeros_like(pb_ref)

    @pl.when(e < last)
    def _():
        for c in range(n_sub):
            apply(vta_ref, pb_ref, c)
            evaluate(pa_ref, 0, c)
        for c in range(n_sub):
            apply(vtb_ref, pa_ref, c)
            evaluate(pb_ref, half, c)

    @pl.when(e == last)
    def _():
        for c in range(n_sub):
            apply(vta_ref, pb_ref, c)


def _peer_experts(xt, u, vt, r2, e2, c1, e1, tb=512, eb=1024, lane_chunk=256, sub_experts=256):
    d, n = xt.shape
    ne = u.shape[0] // eb
    heads, nk, _ = r2.shape
    tb = min(tb, n)
    rows_i = eb // nk
    half = eb // 2
    tok = lambda t, e: (0, 0, t)
    cur = lambda e: jnp.minimum(e, ne - 1)
    return pl.pallas_call(
        functools.partial(_peer_expert_kernel, lane_chunk=min(lane_chunk, tb), sub_experts=sub_experts),
        grid=(n // tb, ne + 1),
        in_specs=[pl.BlockSpec((d, tb), lambda t, e: (0, t)),
                  pl.BlockSpec((eb, d), lambda t, e: (cur(e), 0)),
                  pl.BlockSpec((d, half), lambda t, e: (0, jnp.maximum(2 * e - 1, 0))),
                  pl.BlockSpec((d, half), lambda t, e: (0, 2 * cur(e))),
                  pl.BlockSpec((heads, nk, tb), tok),
                  pl.BlockSpec((heads, nk, tb), tok),
                  pl.BlockSpec((heads, rows_i, tb), lambda t, e: (0, cur(e), t)),
                  pl.BlockSpec((heads, rows_i, tb), lambda t, e: (0, cur(e), t))],
        out_specs=pl.BlockSpec((d, tb), lambda t, e: (0, t)),
        out_shape=jax.ShapeDtypeStruct((d, n), F32),
        scratch_shapes=[pltpu.VMEM((half, tb), BF16), pltpu.VMEM((half, tb), BF16)],
        compiler_params=_params("arbitrary", "arbitrary"),
        name="peer_experts",
    )(xt, u, vt, vt, r2, e2, c1, e1)


def _residual_ln_t_kernel(h_ref, yt_ref, g_ref, b_ref, o_ref, ob_ref):
    out = _layer_norm(ALPHA * h_ref[...] + yt_ref[...].T, g_ref[...], b_ref[...])
    o_ref[...] = out
    ob_ref[...] = out.astype(BF16)


def _residual_ln_t(h, yt, g, b, tm=512):
    n, d = h.shape
    tm = min(tm, n)
    row = lambda i: (i, 0)
    fixed = lambda i: (0, 0)
    return pl.pallas_call(
        _residual_ln_t_kernel,
        grid=(n // tm,),
        in_specs=[pl.BlockSpec((tm, d), row), pl.BlockSpec((d, tm), lambda i: (0, i)),
                  pl.BlockSpec((1, d), fixed), pl.BlockSpec((1, d), fixed)],
        out_specs=[pl.BlockSpec((tm, d), row), pl.BlockSpec((tm, d), row)],
        out_shape=[jax.ShapeDtypeStruct((n, d), F32), jax.ShapeDtypeStruct((n, d), BF16)],
        compiler_params=_params("arbitrary"),
        name="residual_ln_t",
    )(h, yt, g.reshape(1, d), b.reshape(1, d))


def _peer_layer(h, hb, w_q, sub_keys, u_tab, v_tab, ln_g, ln_b):
    q = _matmul(hb, w_q.astype(BF16), BF16, name="peer_query")
    r2, c1, e1, e2 = _peer_route(q, sub_keys.astype(BF16))
    yt = _peer_experts(hb.T, u_tab.astype(BF16), v_tab.astype(BF16).T, r2, e2, c1, e1)
    return _residual_ln_t(h, yt, ln_g, ln_b)


def _sb_mixer(hb, batch, seq, w_in, w_out):
    qkv = _matmul(hb, w_in.astype(BF16), BF16, name="sb_in")
    return _sb_attention(qkv, batch, seq), w_out


def _ml_mixer(hb, batch, seq, w_in, b_gates, norm_g, w_out):
    heads = ML_HEADS
    d_gate = w_out.shape[0]
    n_qkv = w_in.shape[1] - d_gate - 2 * heads
    qkv = _matmul(hb, w_in[:, :n_qkv].astype(BF16), BF16, name="ml_in_qkv")
    w_og = jnp.pad(w_in[:, n_qkv:], ((0, 0), (0, LANES - 2 * heads))).astype(BF16)
    og = _matmul(hb, w_og, F32, tm=512, tn=w_og.shape[1], name="ml_in_gates")
    bias = jnp.pad(b_gates.astype(F32), (0, LANES - 2 * heads)).reshape(1, LANES)
    return _mlstm(qkv, og, bias, norm_g, batch, seq), w_out


def _sg_mixer(hb, batch, seq, w_in, norm_g, norm_b, w_s, b_s, w_out):
    uv = _matmul(hb, w_in.astype(BF16), F32, act="gelu", name="sg_in")
    return _sgu(uv, norm_g, norm_b, w_s, b_s), w_out


def kernel(x, l0_sb_w_in, l0_sb_w_out, l0_ln1_g, l0_ln1_b, l0_peer_w_q, l0_peer_sub_keys, l0_peer_u, l0_peer_v, l0_ln2_g, l0_ln2_b, l1_ml_w_in, l1_ml_b_gates, l1_ml_norm_g, l1_ml_w_out, l1_ln1_g, l1_ln1_b, l1_peer_w_q, l1_peer_sub_keys, l1_peer_u, l1_peer_v, l1_ln2_g, l1_ln2_b, l2_sg_w_in, l2_sg_norm_g, l2_sg_norm_b, l2_sg_w_s, l2_sg_b_s, l2_sg_w_out, l2_ln1_g, l2_ln1_b, l2_peer_w_q, l2_peer_sub_keys, l2_peer_u, l2_peer_v, l2_ln2_g, l2_ln2_b, l3_sb_w_in, l3_sb_w_out, l3_ln1_g, l3_ln1_b, l3_peer_w_q, l3_peer_sub_keys, l3_peer_u, l3_peer_v, l3_ln2_g, l3_ln2_b):
    batch, seq, d = x.shape
    mixers = (
        (_sb_mixer, (l0_sb_w_in, l0_sb_w_out)),
        (_ml_mixer, (l1_ml_w_in, l1_ml_b_gates, l1_ml_norm_g, l1_ml_w_out)),
        (_sg_mixer, (l2_sg_w_in, l2_sg_norm_g, l2_sg_norm_b, l2_sg_w_s, l2_sg_b_s, l2_sg_w_out)),
        (_sb_mixer, (l3_sb_w_in, l3_sb_w_out)),
    )
    norm1 = ((l0_ln1_g, l0_ln1_b), (l1_ln1_g, l1_ln1_b), (l2_ln1_g, l2_ln1_b), (l3_ln1_g, l3_ln1_b))
    peers = (
        (l0_peer_w_q, l0_peer_sub_keys, l0_peer_u, l0_peer_v, l0_ln2_g, l0_ln2_b),
        (l1_peer_w_q, l1_peer_sub_keys, l1_peer_u, l1_peer_v, l1_ln2_g, l1_ln2_b),
        (l2_peer_w_q, l2_peer_sub_keys, l2_peer_u, l2_peer_v, l2_ln2_g, l2_ln2_b),
        (l3_peer_w_q, l3_peer_sub_keys, l3_peer_u, l3_peer_v, l3_ln2_g, l3_ln2_b),
    )
    h = x.reshape(batch * seq, d)
    hb = h.astype(BF16)
    for (mixer, mixer_params), (g1, b1), peer_params in zip(mixers, norm1, peers):
        pre, w_out = mixer(hb, batch, seq, *mixer_params)
        h, hb = _proj_residual_ln(pre, w_out.astype(BF16), h, g1, b1)
        h, hb = _peer_layer(h, hb, *peer_params)
    return h.reshape(batch, seq, d)
```

```python
import functools

import jax
import jax.numpy as jnp
from jax import lax
from jax.experimental import pallas as pl
from jax.experimental.pallas import tpu as pltpu

F32 = jnp.float32
BF16 = jnp.bfloat16

LANES = 128
SUBLANES = 8
BF16_SUBLANES = 16
CHUNK = 128
SB_HEADS = 16
ML_HEADS = 8
SG_GROUPS = 8
PEER_HEADS = 8
PEER_KEYS = 128
PEER_TOPK = 16
DEPTH = 4
ALPHA = (2 * DEPTH) ** 0.25
LN_EPS = 1e-5
VMEM_LIMIT_BYTES = 56 * 1024 * 1024
EXP_ZERO_BELOW = -110.0
NOT_RANKED = 99.0


def _params(*sem):
    return pltpu.CompilerParams(dimension_semantics=sem, vmem_limit_bytes=VMEM_LIMIT_BYTES)


def _gelu(x):
    return 0.5 * x * (1.0 + lax.erf(x * 0.7071067811865476))


def _softplus(z):
    return jnp.maximum(z, 0.0) + jnp.log1p(jnp.exp(-jnp.abs(z)))


def _log_sigmoid(z):
    return -_softplus(-z)


def _layer_norm(t, g, b):
    mu = jnp.mean(t, axis=-1, keepdims=True)
    d = t - mu
    var = jnp.mean(d * d, axis=-1, keepdims=True)
    return d * lax.rsqrt(var + LN_EPS) * g + b


def _dot(a, b):
    return jnp.dot(a, b, preferred_element_type=F32)


def _dot_nt(a, b):
    return lax.dot_general(a, b, (((1,), (1,)), ((), ())), preferred_element_type=F32)


def _dot_tn(a, b):
    return lax.dot_general(a, b, (((0,), (0,)), ((), ())), preferred_element_type=F32)


def _mm_kernel(x_ref, w_ref, o_ref, *, act):
    y = _dot(x_ref[...], w_ref[...])
    if act == "gelu":
        y = _gelu(y)
    o_ref[...] = y.astype(o_ref.dtype)


def _matmul(x, w, out_dtype, act=None, tm=1024, tn=1024, name="matmul"):
    m, k = x.shape
    n = w.shape[1]
    tm, tn = min(tm, m), min(tn, n)
    assert m % tm == 0 and n % tn == 0, (m, n, tm, tn)
    return pl.pallas_call(
        functools.partial(_mm_kernel, act=act),
        grid=(n // tn, m // tm),
        in_specs=[pl.BlockSpec((tm, k), lambda j, i: (i, 0)),
                  pl.BlockSpec((k, tn), lambda j, i: (0, j))],
        out_specs=pl.BlockSpec((tm, tn), lambda j, i: (i, j)),
        out_shape=jax.ShapeDtypeStruct((m, n), out_dtype),
        compiler_params=_params("arbitrary", "arbitrary"),
        name=name,
    )(x, w)


def _proj_ln_kernel(x_ref, w_ref, h_ref, g_ref, b_ref, o_ref, ob_ref):
    y = _dot(x_ref[...], w_ref[...])
    out = _layer_norm(ALPHA * h_ref[...] + y, g_ref[...], b_ref[...])
    o_ref[...] = out
    ob_ref[...] = out.astype(BF16)


def _proj_residual_ln(x, w, h, g, b, tm=512):
    m, k = x.shape
    d = w.shape[1]
    tm = min(tm, m)
    assert m % tm == 0
    row = lambda i: (i, 0)
    fixed = lambda i: (0, 0)
    return pl.pallas_call(
        _proj_ln_kernel,
        grid=(m // tm,),
        in_specs=[pl.BlockSpec((tm, k), row), pl.BlockSpec((k, d), fixed),
                  pl.BlockSpec((tm, d), row), pl.BlockSpec((1, d), fixed),
                  pl.BlockSpec((1, d), fixed)],
        out_specs=[pl.BlockSpec((tm, d), row), pl.BlockSpec((tm, d), row)],
        out_shape=[jax.ShapeDtypeStruct((m, d), F32), jax.ShapeDtypeStruct((m, d), BF16)],
        compiler_params=_params("arbitrary"),
        name="proj_residual_ln",
    )(x, w, h, g.reshape(1, d), b.reshape(1, d))


def _sb_kernel(q_ref, k_ref, v_ref, o_ref, *, blk, win, dh, scale):
    qi = pl.program_id(2)
    heads = q_ref.shape[1] // dh
    row = lax.broadcasted_iota(jnp.int32, (win, win), 0)
    col = lax.broadcasted_iota(jnp.int32, (win, win), 1)
    suffix = jnp.concatenate(
        [(row > col).astype(BF16), jnp.ones((win, LANES), BF16)], axis=1)
    q_pos = qi * blk + lax.broadcasted_iota(jnp.int32, (blk, win), 0)
    offset = lax.broadcasted_iota(jnp.int32, (blk, win), 1)

    def cond(c):
        return jnp.logical_and(c[0] > 0, c[1] > EXP_ZERO_BELOW)

    def body(c):
        end, _, carries, accs = c
        start = pl.multiple_of(jnp.maximum(end - win, 0), blk)
        key_pos = start + offset
        valid = jnp.logical_and(key_pos < q_pos, key_pos < end)
        log_beta, log_keep = [], []
        for h in range(heads):
            cols = slice(h * dh, (h + 1) * dh)
            z = _dot_nt(q_ref[:, cols], k_ref[pl.ds(start, win), cols]) * scale
            sp = _softplus(z)
            log_beta.append(z - sp)
            log_keep.append(jnp.where(valid, -sp, 0.0))
        keep = jnp.concatenate(log_keep, axis=0)
        hi = keep.astype(BF16)
        lo = (keep - hi.astype(F32)).astype(BF16)
        sums = _dot(jnp.concatenate([hi, lo], axis=0), suffix)
        sums = sums[:heads * blk] + sums[heads * blk:]
        new_carries, new_accs = [], []
        for h in range(heads):
            cols = slice(h * dh, (h + 1) * dh)
            part = sums[h * blk:(h + 1) * blk]
            between = part[:, :win] + jnp.concatenate([carries[h]] * (win // LANES), axis=1)
            w = jnp.where(valid, jnp.exp(log_beta[h] + between), 0.0)
            new_carries.append(carries[h] + part[:, win:])
            new_accs.append(accs[h] + _dot(w.astype(BF16), v_ref[pl.ds(start, win), cols]))
        largest = jnp.max(functools.reduce(jnp.maximum, new_carries))
        return start, largest, tuple(new_carries), tuple(new_accs)

    init = ((qi + 1) * blk, jnp.float32(0.0), (jnp.zeros((blk, LANES), F32),) * heads,
            (jnp.zeros((blk, dh), F32),) * heads)
    accs = lax.while_loop(cond, body, init)[3]
    for h in range(heads):
        o_ref[:, h * dh:(h + 1) * dh] = accs[h].astype(o_ref.dtype)


def _sb_attention(qkv, batch, seq, blk=CHUNK, win=3 * CHUNK, heads_per_step=8):
    n, three_d = qkv.shape
    d = three_d // 3
    dh = d // SB_HEADS
    groups = SB_HEADS // heads_per_step
    width = heads_per_step * dh
    nq = seq // blk
    win = min(win, seq)
    return pl.pallas_call(
        functools.partial(_sb_kernel, blk=blk, win=win, dh=dh, scale=dh ** -0.5),
        grid=(batch, groups, nq),
        in_specs=[pl.BlockSpec((blk, width), lambda b, g, i: (b * nq + i, g)),
                  pl.BlockSpec((seq, width), lambda b, g, i: (b, groups + g)),
                  pl.BlockSpec((seq, width), lambda b, g, i: (b, 2 * groups + g))],
        out_specs=pl.BlockSpec((blk, width), lambda b, g, i: (b * nq + i, g)),
        out_shape=jax.ShapeDtypeStruct((n, d), BF16),
        compiler_params=_params("arbitrary", "arbitrary", "arbitrary"),
        name="sb_attention",
    )(qkv, qkv, qkv)


def _mlstm_kernel(q_ref, k_ref, v_ref, o_ref, g_ref, bias_ref, ng_ref, y_ref, c_ref, m_ref,
                  *, dk, dv, scale):
    L = q_ref.shape[0]
    heads = ML_HEADS

    @pl.when(pl.program_id(1) == 0)
    def _():
        c_ref[...] = jnp.zeros_like(c_ref)
        m_ref[...] = jnp.zeros_like(m_ref)

    g = g_ref[...] + bias_ref[...]
    log_f = _log_sigmoid(g)
    row = lax.broadcasted_iota(jnp.int32, (L, L), 0)
    col = lax.broadcasted_iota(jnp.int32, (L, L), 1)
    causal = col <= row
    b_cols = jnp.dot(causal.astype(F32), log_f, precision=lax.Precision.HIGHEST,
                     preferred_element_type=F32)
    b_rows = b_cols.T
    g_rows = g.T
    ones_col = (lax.broadcasted_iota(jnp.int32, (L, LANES), 1) == 0).astype(BF16)

    for h in range(heads):
        q = q_ref[:, h * dk:(h + 1) * dk]
        k = k_ref[:, h * dk:(h + 1) * dk]
        v_ext = jnp.concatenate([v_ref[:, h * dv:(h + 1) * dv], ones_col], axis=1)
        li_col = g[:, h:h + 1]
        li_row = g_rows[h:h + 1, :]
        b_col = b_cols[:, heads + h:heads + h + 1]
        b_row = b_rows[heads + h:heads + h + 1, :]
        m_prev = m_ref[h:h + 1, 0:1]
        c_ext = c_ref[h]

        d_intra = jnp.where(causal, b_col - b_row + li_row, -jnp.inf)
        d_inter = b_col + m_prev
        m_t = jnp.maximum(d_inter, jnp.max(d_intra, axis=-1, keepdims=True))
        w_intra = jnp.exp(d_intra - m_t)
        w_inter = jnp.exp(d_inter - m_t)
        s = _dot_nt(q, k) * scale * w_intra
        num = _dot(s.astype(BF16), v_ext) + w_inter * (_dot(q, c_ext.astype(BF16)) * scale)
        den = num[:, dv:dv + 1]
        hid = num[:, :dv] / jnp.maximum(jnp.abs(den), jnp.exp(-m_t))

        b_last = b_col[L - 1:L, :]
        m_new = jnp.maximum(b_last + m_prev,
                            jnp.max(b_last - b_row + li_row, axis=-1, keepdims=True))
        decay = jnp.exp(b_last + m_prev - m_new)
        w_state = jnp.exp(b_last - b_col + li_col - m_new)
        kw = (k.astype(F32) * w_state).astype(BF16)
        c_ref[h] = decay * c_ext + _dot_tn(kw, v_ext)
        m_ref[h:h + 1, :] = jnp.broadcast_to(m_new, (1, LANES))

        hid = hid * lax.rsqrt(jnp.mean(hid * hid, axis=-1, keepdims=True) + LN_EPS)
        hid = hid * ng_ref[:, h * dv:(h + 1) * dv]
        gate = jax.nn.sigmoid(o_ref[:, h * dv:(h + 1) * dv])
        y_ref[:, h * dv:(h + 1) * dv] = (gate * hid).astype(y_ref.dtype)


def _mlstm(qkv, og, bias, norm_g, batch, seq):
    n = qkv.shape[0]
    heads = ML_HEADS
    d = og.shape[1] - LANES
    dv = d // heads
    dk = dv // 2
    L = CHUNK
    nc = seq // L
    tok = lambda b, c: (b * nc + c, 0)
    fixed = lambda b, c: (0, 0)
    return pl.pallas_call(
        functools.partial(_mlstm_kernel, dk=dk, dv=dv, scale=dk ** -0.5),
        grid=(batch, nc),
        in_specs=[pl.BlockSpec((L, heads * dk), tok),
                  pl.BlockSpec((L, heads * dk), lambda b, c: (b * nc + c, 1)),
                  pl.BlockSpec((L, d), lambda b, c: (b * nc + c, 1)),
                  pl.BlockSpec((L, d), tok),
                  pl.BlockSpec((L, LANES), lambda b, c: (b * nc + c, d // LANES)),
                  pl.BlockSpec((1, LANES), fixed),
                  pl.BlockSpec((1, d), fixed)],
        out_specs=pl.BlockSpec((L, d), tok),
        out_shape=jax.ShapeDtypeStruct((n, d), BF16),
        scratch_shapes=[pltpu.VMEM((heads, dk, dv + LANES), F32),
                        pltpu.VMEM((heads, LANES), F32)],
        compiler_params=_params("arbitrary", "arbitrary"),
        name="mlstm",
    )(qkv, qkv, qkv, og, og, bias, norm_g.reshape(1, d))


def _sgu_kernel(u_ref, v_ref, g_ref, b_ref, ws_ref, bs_ref, y_ref):
    L, width = u_ref.shape
    dg = width // SG_GROUPS
    vn = _layer_norm(v_ref[...], g_ref[...], b_ref[...]).astype(BF16)
    row = lax.broadcasted_iota(jnp.int32, (L, L), 0)
    col = lax.broadcasted_iota(jnp.int32, (L, L), 1)
    for g in range(SG_GROUPS):
        ws = jnp.where(col <= row, ws_ref[g], 0.0).astype(BF16)
        mixed = _dot(ws, vn[:, g * dg:(g + 1) * dg]) + bs_ref[:, g:g + 1]
        y_ref[:, g * dg:(g + 1) * dg] = (u_ref[:, g * dg:(g + 1) * dg] * mixed).astype(y_ref.dtype)


def _sgu(uv, norm_g, norm_b, w_s, b_s):
    n, two_w = uv.shape
    width = two_w // 2
    L = CHUNK
    bs_t = jnp.zeros((L, LANES), F32).at[:, :SG_GROUPS].set(b_s.T)
    fixed = lambda i: (0, 0)
    return pl.pallas_call(
        _sgu_kernel,
        grid=(n // L,),
        in_specs=[pl.BlockSpec((L, width), lambda i: (i, 0)),
                  pl.BlockSpec((L, width), lambda i: (i, 1)),
                  pl.BlockSpec((1, width), fixed), pl.BlockSpec((1, width), fixed),
                  pl.BlockSpec((SG_GROUPS, L, L), lambda i: (0, 0, 0)),
                  pl.BlockSpec((L, LANES), fixed)],
        out_specs=pl.BlockSpec((L, width), lambda i: (i, 0)),
        out_shape=jax.ShapeDtypeStruct((n, width), BF16),
        compiler_params=_params("arbitrary"),
        name="sgu",
    )(uv, uv, norm_g.reshape(1, width), norm_b.reshape(1, width), w_s, bs_t)


def _extract_top(x, count, first_only):
    rows = x.shape[0]
    iota = lax.broadcasted_iota(jnp.int32, x.shape, 0)
    rank = jnp.full(x.shape, NOT_RANKED, F32)
    vals = []
    for k in range(count):
        mx = jnp.max(x, axis=0, keepdims=True)
        sel = x == mx
        if first_only:
            sel = iota == jnp.min(jnp.where(sel, iota, rows), axis=0, keepdims=True)
        vals.append(mx)
        rank = jnp.where(sel, float(k + 1), rank)
        x = jnp.where(sel, -jnp.inf, x)
    return vals, rank


def _candidate_layout(top):
    tiles, cur, fill = [], [], 0
    for p in range(top):
        n, q0 = top // (p + 1), 0
        while n > 0:
            if fill == SUBLANES or (n < SUBLANES and fill + n > SUBLANES):
                tiles.append(cur)
                cur, fill = [], 0
            take = min(n, SUBLANES - fill)
            cur.append((p, q0, fill, take))
            fill, q0, n = fill + take, q0 + take, n - take
    tiles.append(cur)
    return [t for t in tiles if t]


def _count_ranked(rank):
    return jnp.sum((rank < NOT_RANKED).astype(F32), axis=0, keepdims=True)


def _route_one_head(s1, s2, first_only):
    K = PEER_TOPK
    T = s1.shape[1]
    a, rank1 = _extract_top(s1, K, first_only)
    b, rank2 = _extract_top(s2, K, first_only)

    b_all = jnp.concatenate(b, axis=0)
    sub = lax.broadcasted_iota(jnp.int32, (SUBLANES, T), 0)
    layout = _candidate_layout(K)
    tiles = []
    for segments in layout:
        tile = jnp.full((SUBLANES, T), -jnp.inf, F32)
        for p, q0, off, n in segments:
            src = b_all[q0:q0 + SUBLANES]
            if off:
                src = pltpu.roll(src, off, 0)
            rows = jnp.logical_and(sub >= off, sub < off + n)
            tile = jnp.where(rows, a[p] + src, tile)
        tiles.append(tile)
    cand = jnp.concatenate(tiles, axis=0)
    best, order = _extract_top(cand, K, first_only)
    chosen = (order < NOT_RANKED).astype(F32)
    counts = [jnp.zeros((1, T), F32) for _ in range(K)]
    for v, segments in enumerate(layout):
        part = chosen[v * SUBLANES:(v + 1) * SUBLANES]
        for p, _, off, n in segments:
            rows = jnp.logical_and(sub >= off, sub < off + n)
            counts[p] = counts[p] + jnp.sum(jnp.where(rows, part, 0.0), axis=0, keepdims=True)
    ranked = jnp.maximum(jnp.maximum(_count_ranked(rank1), _count_ranked(rank2)),
                         functools.reduce(jnp.add, counts))

    z = jnp.zeros_like(best[0])
    for k in range(K):
        z = z + jnp.exp(best[k] - best[0])
    c1 = jnp.zeros_like(s1)
    for p in range(K):
        c1 = jnp.where(rank1 == float(p + 1), counts[p], c1)
    return rank2, c1, jnp.exp(s1 - a[0]) / z, jnp.exp(s2 - b[0]), ranked


def _peer_route_kernel(q_ref, keys_ref, r2_ref, c1_ref, e1_ref, e2_ref):
    half = keys_ref.shape[3]

    def route(first_only):
        ranked = []
        for h in range(keys_ref.shape[0]):
            s1 = _dot_nt(keys_ref[h, 0], q_ref[:, 2 * h * half:(2 * h + 1) * half])
            s2 = _dot_nt(keys_ref[h, 1], q_ref[:, (2 * h + 1) * half:(2 * h + 2) * half])
            r2, c1_ref[h], e1_ref[h], e2, n = _route_one_head(s1, s2, first_only)
            r2_ref[h] = r2.astype(r2_ref.dtype)
            e2_ref[h] = e2.astype(e2_ref.dtype)
            ranked.append(n)
        return jnp.max(functools.reduce(jnp.maximum, ranked))

    most_ranked = route(False)

    @pl.when(most_ranked > PEER_TOPK)
    def _():
        route(True)


def _peer_route(q, keys, tt=LANES, heads_per_step=4):
    n = q.shape[0]
    heads, _, nk, half = keys.shape
    hp = heads_per_step
    out = lambda dtype: jax.ShapeDtypeStruct((heads, nk, n), dtype)
    spec = pl.BlockSpec((hp, nk, tt), lambda i, g: (g, 0, i))
    return pl.pallas_call(
        _peer_route_kernel,
        grid=(n // tt, heads // hp),
        in_specs=[pl.BlockSpec((tt, 2 * half * hp), lambda i, g: (i, g)),
                  pl.BlockSpec((hp, 2, nk, half), lambda i, g: (g, 0, 0, 0))],
        out_specs=[spec] * 4,
        out_shape=[out(BF16), out(F32), out(F32), out(BF16)],
        compiler_params=_params("arbitrary", "arbitrary"),
        name="peer_route",
    )(q, keys)


def _peer_expert_kernel(x_ref, u_ref, vta_ref, vtb_ref, r2_ref, e2_ref, c1_ref, e1_ref, o_ref,
                        pa_ref, pb_ref, *, lane_chunk, sub_experts):
    e = pl.program_id(1)
    last = pl.num_programs(1) - 1
    nk = PEER_KEYS
    tb = x_ref.shape[0]
    half = u_ref.shape[0] // 2
    n_sub = half // sub_experts
    d_rows = vta_ref.shape[0] // n_sub

    def rows_to_tile(ref, h, ii, ts):
        row = jnp.broadcast_to(ref[h, ii:ii + 1, ts], (BF16_SUBLANES, lane_chunk)).astype(BF16)
        return jnp.concatenate([row] * (nk // BF16_SUBLANES), axis=0)

    def evaluate(p_ref, first, c):
        rows = slice(first + c * sub_experts, first + (c + 1) * sub_experts)
        act = _gelu(_dot_nt(u_ref[rows, :], x_ref[...])).astype(BF16)
        for r0 in range(0, sub_experts, nk):
            ii = (rows.start + r0) // nk
            for t0 in range(0, tb, lane_chunk):
                ts = slice(t0, t0 + lane_chunk)
                gate = jnp.zeros((nk, lane_chunk), BF16)
                for h in range(PEER_HEADS):
                    picked = r2_ref[h, :, ts] <= rows_to_tile(c1_ref, h, ii, ts)
                    gate = gate + jnp.where(picked, e2_ref[h, :, ts], 0.0) * rows_to_tile(e1_ref, h, ii, ts)
                p_ref[c * sub_experts + r0:c * sub_experts + r0 + nk, ts] = gate * act[r0:r0 + nk, ts]

    def apply(vt_ref, p_ref, c):
        rows = slice(c * d_rows, (c + 1) * d_rows)
        o_ref[rows, :] += _dot(vt_ref[rows, :], p_ref[...])

    @pl.when(e == 0)
    def _():
        o_ref[...] = jnp.zeros_like(o_ref)
        pb_ref[...] = jnp.zeros_like(pb_ref)

    @pl.when(e < last)
    def _():
        for c in range(n_sub):
            evaluate(pa_ref, 0, c)
            apply(vta_ref, pb_ref, c)
        for c in range(n_sub):
            evaluate(pb_ref, half, c)
            apply(vtb_ref, pa_ref, c)

    @pl.when(e == last)
    def _():
        for c in range(n_sub):
            apply(vta_ref, pb_ref, c)


def _peer_experts(x, u, vt, r2, e2, c1, e1, tb=512, eb=1024, lane_chunk=256, sub_experts=256):
    n, d = x.shape
    ne = u.shape[0] // eb
    heads, nk, _ = r2.shape
    tb = min(tb, n)
    rows_i = eb // nk
    half = eb // 2
    tok = lambda t, e: (0, 0, t)
    cur = lambda e: jnp.minimum(e, ne - 1)
    return pl.pallas_call(
        functools.partial(_peer_expert_kernel, lane_chunk=min(lane_chunk, tb), sub_experts=sub_experts),
        grid=(n // tb, ne + 1),
        in_specs=[pl.BlockSpec((tb, d), lambda t, e: (t, 0)),
                  pl.BlockSpec((eb, d), lambda t, e: (cur(e), 0)),
                  pl.BlockSpec((d, half), lambda t, e: (0, jnp.maximum(2 * e - 1, 0))),
                  pl.BlockSpec((d, half), lambda t, e: (0, 2 * cur(e))),
                  pl.BlockSpec((heads, nk, tb), tok),
                  pl.BlockSpec((heads, nk, tb), tok),
                  pl.BlockSpec((heads, rows_i, tb), lambda t, e: (0, cur(e), t)),
                  pl.BlockSpec((heads, rows_i, tb), lambda t, e: (0, cur(e), t))],
        out_specs=pl.BlockSpec((d, tb), lambda t, e: (0, t)),
        out_shape=jax.ShapeDtypeStruct((d, n), F32),
        scratch_shapes=[pltpu.VMEM((half, tb), BF16), pltpu.VMEM((half, tb), BF16)],
        compiler_params=_params("arbitrary", "arbitrary"),
        name="peer_experts",
    )(x, u, vt, vt, r2, e2, c1, e1)


def _residual_ln_t_kernel(h_ref, yt_ref, g_ref, b_ref, o_ref, ob_ref):
    out = _layer_norm(ALPHA * h_ref[...] + yt_ref[...].T, g_ref[...], b_ref[...])
    o_ref[...] = out
    ob_ref[...] = out.astype(BF16)


def _residual_ln_t(h, yt, g, b, tm=512):
    n, d = h.shape
    tm = min(tm, n)
    row = lambda i: (i, 0)
    fixed = lambda i: (0, 0)
    return pl.pallas_call(
        _residual_ln_t_kernel,
        grid=(n // tm,),
        in_specs=[pl.BlockSpec((tm, d), row), pl.BlockSpec((d, tm), lambda i: (0, i)),
                  pl.BlockSpec((1, d), fixed), pl.BlockSpec((1, d), fixed)],
        out_specs=[pl.BlockSpec((tm, d), row), pl.BlockSpec((tm, d), row)],
        out_shape=[jax.ShapeDtypeStruct((n, d), F32), jax.ShapeDtypeStruct((n, d), BF16)],
        compiler_params=_params("arbitrary"),
        name="residual_ln_t",
    )(h, yt, g.reshape(1, d), b.reshape(1, d))


def _peer_layer(h, hb, w_q, sub_keys, u_tab, v_tab, ln_g, ln_b):
    q = _matmul(hb, w_q.astype(BF16), BF16, name="peer_query")
    r2, c1, e1, e2 = _peer_route(q, sub_keys.astype(BF16))
    yt = _peer_experts(hb, u_tab.astype(BF16), v_tab.astype(BF16).T, r2, e2, c1, e1)
    return _residual_ln_t(h, yt, ln_g, ln_b)


def _sb_mixer(hb, batch, seq, w_in, w_out):
    qkv = _matmul(hb, w_in.astype(BF16), BF16, name="sb_in")
    return _sb_attention(qkv, batch, seq), w_out


def _ml_mixer(hb, batch, seq, w_in, b_gates, norm_g, w_out):
    heads = ML_HEADS
    d_gate = w_out.shape[0]
    n_qkv = w_in.shape[1] - d_gate - 2 * heads
    qkv = _matmul(hb, w_in[:, :n_qkv].astype(BF16), BF16, name="ml_in_qkv")
    w_og = jnp.pad(w_in[:, n_qkv:], ((0, 0), (0, LANES - 2 * heads))).astype(BF16)
    og = _matmul(hb, w_og, F32, tm=512, tn=w_og.shape[1], name="ml_in_gates")
    bias = jnp.pad(b_gates.astype(F32), (0, LANES - 2 * heads)).reshape(1, LANES)
    return _mlstm(qkv, og, bias, norm_g, batch, seq), w_out


def _sg_mixer(hb, batch, seq, w_in, norm_g, norm_b, w_s, b_s, w_out):
    uv = _matmul(hb, w_in.astype(BF16), F32, act="gelu", name="sg_in")
    return _sgu(uv, norm_g, norm_b, w_s, b_s), w_out


def kernel(x, l0_sb_w_in, l0_sb_w_out, l0_ln1_g, l0_ln1_b, l0_peer_w_q, l0_peer_sub_keys, l0_peer_u, l0_peer_v, l0_ln2_g, l0_ln2_b, l1_ml_w_in, l1_ml_b_gates, l1_ml_norm_g, l1_ml_w_out, l1_ln1_g, l1_ln1_b, l1_peer_w_q, l1_peer_sub_keys, l1_peer_u, l1_peer_v, l1_ln2_g, l1_ln2_b, l2_sg_w_in, l2_sg_norm_g, l2_sg_norm_b, l2_sg_w_s, l2_sg_b_s, l2_sg_w_out, l2_ln1_g, l2_ln1_b, l2_peer_w_q, l2_peer_sub_keys, l2_peer_u, l2_peer_v, l2_ln2_g, l2_ln2_b, l3_sb_w_in, l3_sb_w_out, l3_ln1_g, l3_ln1_b, l3_peer_w_q, l3_peer_sub_keys, l3_peer_u, l3_peer_v, l3_ln2_g, l3_ln2_b):
    batch, seq, d = x.shape
    mixers = (
        (_sb_mixer, (l0_sb_w_in, l0_sb_w_out)),
        (_ml_mixer, (l1_ml_w_in, l1_ml_b_gates, l1_ml_norm_g, l1_ml_w_out)),
        (_sg_mixer, (l2_sg_w_in, l2_sg_norm_g, l2_sg_norm_b, l2_sg_w_s, l2_sg_b_s, l2_sg_w_out)),
        (_sb_mixer, (l3_sb_w_in, l3_sb_w_out)),
    )
    norm1 = ((l0_ln1_g, l0_ln1_b), (l1_ln1_g, l1_ln1_b), (l2_ln1_g, l2_ln1_b), (l3_ln1_g, l3_ln1_b))
    peers = (
        (l0_peer_w_q, l0_peer_sub_keys, l0_peer_u, l0_peer_v, l0_ln2_g, l0_ln2_b),
        (l1_peer_w_q, l1_peer_sub_keys, l1_peer_u, l1_peer_v, l1_ln2_g, l1_ln2_b),
        (l2_peer_w_q, l2_peer_sub_keys, l2_peer_u, l2_peer_v, l2_ln2_g, l2_ln2_b),
        (l3_peer_w_q, l3_peer_sub_keys, l3_peer_u, l3_peer_v, l3_ln2_g, l3_ln2_b),
    )
    h = x.reshape(batch * seq, d)
    hb = h.astype(BF16)
    for (mixer, mixer_params), (g1, b1), peer_params in zip(mixers, norm1, peers):
        pre, w_out = mixer(hb, batch, seq, *mixer_params)
        h, hb = _proj_residual_ln(pre, w_out.astype(BF16), h, g1, b1)
        h, hb = _peer_layer(h, hb, *peer_params)
    return h.reshape(batch, seq, d)
```

```python
import functools

import jax
import jax.numpy as jnp
from jax import lax
from jax.experimental import pallas as pl
from jax.experimental.pallas import tpu as pltpu

F32 = jnp.float32
BF16 = jnp.bfloat16

LANES = 128
SUBLANES = 8
BF16_SUBLANES = 16
CHUNK = 128
SB_HEADS = 16
ML_HEADS = 8
SG_GROUPS = 8
PEER_HEADS = 8
PEER_KEYS = 128
PEER_TOPK = 16
DEPTH = 4
ALPHA = (2 * DEPTH) ** 0.25
LN_EPS = 1e-5
VMEM_LIMIT_BYTES = 56 * 1024 * 1024
EXP_ZERO_BELOW = -110.0
NOT_RANKED = 99.0


def _params(*sem):
    return pltpu.CompilerParams(dimension_semantics=sem, vmem_limit_bytes=VMEM_LIMIT_BYTES)


def _gelu(x):
    return 0.5 * x * (1.0 + lax.erf(x * 0.7071067811865476))


def _softplus(z):
    return jnp.maximum(z, 0.0) + jnp.log1p(jnp.exp(-jnp.abs(z)))


def _log_sigmoid(z):
    return -_softplus(-z)


def _layer_norm(t, g, b):
    mu = jnp.mean(t, axis=-1, keepdims=True)
    d = t - mu
    var = jnp.mean(d * d, axis=-1, keepdims=True)
    return d * lax.rsqrt(var + LN_EPS) * g + b


def _dot(a, b):
    return jnp.dot(a, b, preferred_element_type=F32)


def _dot_nt(a, b):
    return lax.dot_general(a, b, (((1,), (1,)), ((), ())), preferred_element_type=F32)


def _dot_tn(a, b):
    return lax.dot_general(a, b, (((0,), (0,)), ((), ())), preferred_element_type=F32)


def _mm_kernel(x_ref, w_ref, o_ref, *, act):
    y = _dot(x_ref[...], w_ref[...])
    if act == "gelu":
        y = _gelu(y)
    o_ref[...] = y.astype(o_ref.dtype)


def _matmul(x, w, out_dtype, act=None, tm=1024, tn=1024, name="matmul"):
    m, k = x.shape
    n = w.shape[1]
    tm, tn = min(tm, m), min(tn, n)
    assert m % tm == 0 and n % tn == 0, (m, n, tm, tn)
    return pl.pallas_call(
        functools.partial(_mm_kernel, act=act),
        grid=(n // tn, m // tm),
        in_specs=[pl.BlockSpec((tm, k), lambda j, i: (i, 0)),
                  pl.BlockSpec((k, tn), lambda j, i: (0, j))],
        out_specs=pl.BlockSpec((tm, tn), lambda j, i: (i, j)),
        out_shape=jax.ShapeDtypeStruct((m, n), out_dtype),
        compiler_params=_params("arbitrary", "arbitrary"),
        name=name,
    )(x, w)


def _proj_ln_kernel(x_ref, w_ref, h_ref, g_ref, b_ref, o_ref, ob_ref):
    y = _dot(x_ref[...], w_ref[...])
    out = _layer_norm(ALPHA * h_ref[...] + y, g_ref[...], b_ref[...])
    o_ref[...] = out
    ob_ref[...] = out.astype(BF16)


def _proj_residual_ln(x, w, h, g, b, tm=512):
    m, k = x.shape
    d = w.shape[1]
    tm = min(tm, m)
    assert m % tm == 0
    row = lambda i: (i, 0)
    fixed = lambda i: (0, 0)
    return pl.pallas_call(
        _proj_ln_kernel,
        grid=(m // tm,),
        in_specs=[pl.BlockSpec((tm, k), row), pl.BlockSpec((k, d), fixed),
                  pl.BlockSpec((tm, d), row), pl.BlockSpec((1, d), fixed),
                  pl.BlockSpec((1, d), fixed)],
        out_specs=[pl.BlockSpec((tm, d), row), pl.BlockSpec((tm, d), row)],
        out_shape=[jax.ShapeDtypeStruct((m, d), F32), jax.ShapeDtypeStruct((m, d), BF16)],
        compiler_params=_params("arbitrary"),
        name="proj_residual_ln",
    )(x, w, h, g.reshape(1, d), b.reshape(1, d))


def _sb_kernel(q_ref, k_ref, v_ref, o_ref, *, blk, win, dh, scale):
    qi = pl.program_id(2)
    heads = q_ref.shape[1] // dh
    row = lax.broadcasted_iota(jnp.int32, (win, win), 0)
    col = lax.broadcasted_iota(jnp.int32, (win, win), 1)
    suffix = jnp.concatenate(
        [(row > col).astype(BF16), jnp.ones((win, LANES), BF16)], axis=1)
    q_pos = qi * blk + lax.broadcasted_iota(jnp.int32, (blk, win), 0)
    offset = lax.broadcasted_iota(jnp.int32, (blk, win), 1)

    def cond(c):
        return jnp.logical_and(c[0] > 0, c[1] > EXP_ZERO_BELOW)

    def body(c):
        end, _, carries, accs = c
        start = pl.multiple_of(jnp.maximum(end - win, 0), blk)
        key_pos = start + offset
        valid = jnp.logical_and(key_pos < q_pos, key_pos < end)
        log_beta, log_keep = [], []
        for h in range(heads):
            cols = slice(h * dh, (h + 1) * dh)
            z = _dot_nt(q_ref[:, cols], k_ref[pl.ds(start, win), cols]) * scale
            sp = _softplus(z)
            log_beta.append(z - sp)
            log_keep.append(jnp.where(valid, -sp, 0.0))
        keep = jnp.concatenate(log_keep, axis=0)
        hi = keep.astype(BF16)
        lo = (keep - hi.astype(F32)).astype(BF16)
        sums = _dot(jnp.concatenate([hi, lo], axis=0), suffix)
        sums = sums[:heads * blk] + sums[heads * blk:]
        new_carries, new_accs = [], []
        for h in range(heads):
            cols = slice(h * dh, (h + 1) * dh)
            part = sums[h * blk:(h + 1) * blk]
            between = part[:, :win] + jnp.concatenate([carries[h]] * (win // LANES), axis=1)
            w = jnp.where(valid, jnp.exp(log_beta[h] + between), 0.0)
            new_carries.append(carries[h] + part[:, win:])
            new_accs.append(accs[h] + _dot(w.astype(BF16), v_ref[pl.ds(start, win), cols]))
        largest = jnp.max(functools.reduce(jnp.maximum, new_carries))
        return start, largest, tuple(new_carries), tuple(new_accs)

    init = ((qi + 1) * blk, jnp.float32(0.0), (jnp.zeros((blk, LANES), F32),) * heads,
            (jnp.zeros((blk, dh), F32),) * heads)
    accs = lax.while_loop(cond, body, init)[3]
    for h in range(heads):
        o_ref[:, h * dh:(h + 1) * dh] = accs[h].astype(o_ref.dtype)


def _sb_attention(qkv, batch, seq, blk=CHUNK, win=3 * CHUNK, heads_per_step=8):
    n, three_d = qkv.shape
    d = three_d // 3
    dh = d // SB_HEADS
    groups = SB_HEADS // heads_per_step
    width = heads_per_step * dh
    nq = seq // blk
    win = min(win, seq)
    return pl.pallas_call(
        functools.partial(_sb_kernel, blk=blk, win=win, dh=dh, scale=dh ** -0.5),
        grid=(batch, groups, nq),
        in_specs=[pl.BlockSpec((blk, width), lambda b, g, i: (b * nq + i, g)),
                  pl.BlockSpec((seq, width), lambda b, g, i: (b, groups + g)),
                  pl.BlockSpec((seq, width), lambda b, g, i: (b, 2 * groups + g))],
        out_specs=pl.BlockSpec((blk, width), lambda b, g, i: (b * nq + i, g)),
        out_shape=jax.ShapeDtypeStruct((n, d), BF16),
        compiler_params=_params("arbitrary", "arbitrary", "arbitrary"),
        name="sb_attention",
    )(qkv, qkv, qkv)


def _mlstm_kernel(q_ref, k_ref, v_ref, o_ref, g_ref, bias_ref, ng_ref, y_ref, c_ref, m_ref,
                  *, dk, dv, scale):
    L = q_ref.shape[0]
    heads = ML_HEADS

    @pl.when(pl.program_id(1) == 0)
    def _():
        c_ref[...] = jnp.zeros_like(c_ref)
        m_ref[...] = jnp.zeros_like(m_ref)

    g = g_ref[...] + bias_ref[...]
    log_f = _log_sigmoid(g)
    row = lax.broadcasted_iota(jnp.int32, (L, L), 0)
    col = lax.broadcasted_iota(jnp.int32, (L, L), 1)
    causal = col <= row
    b_cols = jnp.dot(causal.astype(F32), log_f, precision=lax.Precision.HIGHEST,
                     preferred_element_type=F32)
    b_rows = b_cols.T
    g_rows = g.T
    ones_col = (lax.broadcasted_iota(jnp.int32, (L, LANES), 1) == 0).astype(BF16)

    for h in range(heads):
        q = q_ref[:, h * dk:(h + 1) * dk]
        k = k_ref[:, h * dk:(h + 1) * dk]
        v_ext = jnp.concatenate([v_ref[:, h * dv:(h + 1) * dv], ones_col], axis=1)
        li_col = g[:, h:h + 1]
        li_row = g_rows[h:h + 1, :]
        b_col = b_cols[:, heads + h:heads + h + 1]
        b_row = b_rows[heads + h:heads + h + 1, :]
        m_prev = m_ref[h:h + 1, 0:1]
        c_ext = c_ref[h]

        d_intra = jnp.where(causal, b_col - b_row + li_row, -jnp.inf)
        d_inter = b_col + m_prev
        m_t = jnp.maximum(d_inter, jnp.max(d_intra, axis=-1, keepdims=True))
        w_intra = jnp.exp(d_intra - m_t)
        w_inter = jnp.exp(d_inter - m_t)
        s = _dot_nt(q, k) * scale * w_intra
        num = _dot(s.astype(BF16), v_ext) + w_inter * (_dot(q, c_ext.astype(BF16)) * scale)
        den = num[:, dv:dv + 1]
        hid = num[:, :dv] / jnp.maximum(jnp.abs(den), jnp.exp(-m_t))

        b_last = b_col[L - 1:L, :]
        m_new = jnp.maximum(b_last + m_prev,
                            jnp.max(b_last - b_row + li_row, axis=-1, keepdims=True))
        decay = jnp.exp(b_last + m_prev - m_new)
        w_state = jnp.exp(b_last - b_col + li_col - m_new)
        kw = (k.astype(F32) * w_state).astype(BF16)
        c_ref[h] = decay * c_ext + _dot_tn(kw, v_ext)
        m_ref[h:h + 1, :] = jnp.broadcast_to(m_new, (1, LANES))

        hid = hid * lax.rsqrt(jnp.mean(hid * hid, axis=-1, keepdims=True) + LN_EPS)
        hid = hid * ng_ref[:, h * dv:(h + 1) * dv]
        gate = jax.nn.sigmoid(o_ref[:, h * dv:(h + 1) * dv])
        y_ref[:, h * dv:(h + 1) * dv] = (gate * hid).astype(y_ref.dtype)


def _mlstm(qkv, og, bias, norm_g, batch, seq):
    n = qkv.shape[0]
    heads = ML_HEADS
    d = og.shape[1] - LANES
    dv = d // heads
    dk = dv // 2
    L = CHUNK
    nc = seq // L
    tok = lambda b, c: (b * nc + c, 0)
    fixed = lambda b, c: (0, 0)
    return pl.pallas_call(
        functools.partial(_mlstm_kernel, dk=dk, dv=dv, scale=dk ** -0.5),
        grid=(batch, nc),
        in_specs=[pl.BlockSpec((L, heads * dk), tok),
                  pl.BlockSpec((L, heads * dk), lambda b, c: (b * nc + c, 1)),
                  pl.BlockSpec((L, d), lambda b, c: (b * nc + c, 1)),
                  pl.BlockSpec((L, d), tok),
                  pl.BlockSpec((L, LANES), lambda b, c: (b * nc + c, d // LANES)),
                  pl.BlockSpec((1, LANES), fixed),
                  pl.BlockSpec((1, d), fixed)],
        out_specs=pl.BlockSpec((L, d), tok),
        out_shape=jax.ShapeDtypeStruct((n, d), BF16),
        scratch_shapes=[pltpu.VMEM((heads, dk, dv + LANES), F32),
                        pltpu.VMEM((heads, LANES), F32)],
        compiler_params=_params("arbitrary", "arbitrary"),
        name="mlstm",
    )(qkv, qkv, qkv, og, og, bias, norm_g.reshape(1, d))


def _sgu_kernel(u_ref, v_ref, g_ref, b_ref, ws_ref, bs_ref, y_ref):
    L, width = u_ref.shape
    dg = width // SG_GROUPS
    vn = _layer_norm(v_ref[...], g_ref[...], b_ref[...]).astype(BF16)
    row = lax.broadcasted_iota(jnp.int32, (L, L), 0)
    col = lax.broadcasted_iota(jnp.int32, (L, L), 1)
    for g in range(SG_GROUPS):
        ws = jnp.where(col <= row, ws_ref[g], 0.0).astype(BF16)
        mixed = _dot(ws, vn[:, g * dg:(g + 1) * dg]) + bs_ref[:, g:g + 1]
        y_ref[:, g * dg:(g + 1) * dg] = (u_ref[:, g * dg:(g + 1) * dg] * mixed).astype(y_ref.dtype)


def _sgu(uv, norm_g, norm_b, w_s, b_s):
    n, two_w = uv.shape
    width = two_w // 2
    L = CHUNK
    bs_t = jnp.zeros((L, LANES), F32).at[:, :SG_GROUPS].set(b_s.T)
    fixed = lambda i: (0, 0)
    return pl.pallas_call(
        _sgu_kernel,
        grid=(n // L,),
        in_specs=[pl.BlockSpec((L, width), lambda i: (i, 0)),
                  pl.BlockSpec((L, width), lambda i: (i, 1)),
                  pl.BlockSpec((1, width), fixed), pl.BlockSpec((1, width), fixed),
                  pl.BlockSpec((SG_GROUPS, L, L), lambda i: (0, 0, 0)),
                  pl.BlockSpec((L, LANES), fixed)],
        out_specs=pl.BlockSpec((L, width), lambda i: (i, 0)),
        out_shape=jax.ShapeDtypeStruct((n, width), BF16),
        compiler_params=_params("arbitrary"),
        name="sgu",
    )(uv, uv, norm_g.reshape(1, width), norm_b.reshape(1, width), w_s, bs_t)


def _extract_top(x, count, first_only):
    rows = x.shape[0]
    iota = lax.broadcasted_iota(jnp.int32, x.shape, 0)
    rank = jnp.full(x.shape, NOT_RANKED, F32)
    vals = []
    for k in range(count):
        mx = jnp.max(x, axis=0, keepdims=True)
        sel = x == mx
        if first_only:
            sel = iota == jnp.min(jnp.where(sel, iota, rows), axis=0, keepdims=True)
        vals.append(mx)
        rank = jnp.where(sel, float(k + 1), rank)
        x = jnp.where(sel, -jnp.inf, x)
    return vals, rank


def _candidate_layout(top):
    tiles, cur, fill = [], [], 0
    for p in range(top):
        n, q0 = top // (p + 1), 0
        while n > 0:
            if fill == SUBLANES or (n < SUBLANES and fill + n > SUBLANES):
                tiles.append(cur)
                cur, fill = [], 0
            take = min(n, SUBLANES - fill)
            cur.append((p, q0, fill, take))
            fill, q0, n = fill + take, q0 + take, n - take
    tiles.append(cur)
    return [t for t in tiles if t]


def _count_ranked(rank):
    return jnp.sum((rank < NOT_RANKED).astype(F32), axis=0, keepdims=True)


def _route_one_head(s1, s2, first_only):
    K = PEER_TOPK
    T = s1.shape[1]
    a, rank1 = _extract_top(s1, K, first_only)
    b, rank2 = _extract_top(s2, K, first_only)

    b_all = jnp.concatenate(b, axis=0)
    sub = lax.broadcasted_iota(jnp.int32, (SUBLANES, T), 0)
    layout = _candidate_layout(K)
    tiles = []
    for segments in layout:
        tile = jnp.full((SUBLANES, T), -jnp.inf, F32)
        for p, q0, off, n in segments:
            src = b_all[q0:q0 + SUBLANES]
            if off:
                src = pltpu.roll(src, off, 0)
            rows = jnp.logical_and(sub >= off, sub < off + n)
            tile = jnp.where(rows, a[p] + src, tile)
        tiles.append(tile)
    cand = jnp.concatenate(tiles, axis=0)
    best, order = _extract_top(cand, K, first_only)
    chosen = (order < NOT_RANKED).astype(F32)
    counts = [jnp.zeros((1, T), F32) for _ in range(K)]
    for v, segments in enumerate(layout):
        part = chosen[v * SUBLANES:(v + 1) * SUBLANES]
        for p, _, off, n in segments:
            rows = jnp.logical_and(sub >= off, sub < off + n)
            counts[p] = counts[p] + jnp.sum(jnp.where(rows, part, 0.0), axis=0, keepdims=True)
    ranked = jnp.maximum(jnp.maximum(_count_ranked(rank1), _count_ranked(rank2)),
                         functools.reduce(jnp.add, counts))

    z = jnp.zeros_like(best[0])
    for k in range(K):
        z = z + jnp.exp(best[k] - best[0])
    c1 = jnp.zeros_like(s1)
    for p in range(K):
        c1 = jnp.where(rank1 == float(p + 1), counts[p], c1)
    return rank2, c1, jnp.exp(s1 - a[0]) / z, jnp.exp(s2 - b[0]), ranked


def _peer_route_kernel(q_ref, keys_ref, r2_ref, c1_ref, e1_ref, e2_ref):
    half = keys_ref.shape[3]

    def route(first_only):
        ranked = []
        for h in range(keys_ref.shape[0]):
            s1 = _dot_nt(keys_ref[h, 0], q_ref[:, 2 * h * half:(2 * h + 1) * half])
            s2 = _dot_nt(keys_ref[h, 1], q_ref[:, (2 * h + 1) * half:(2 * h + 2) * half])
            r2, c1_ref[h], e1_ref[h], e2, n = _route_one_head(s1, s2, first_only)
            r2_ref[h] = r2.astype(r2_ref.dtype)
            e2_ref[h] = e2.astype(e2_ref.dtype)
            ranked.append(n)
        return jnp.max(functools.reduce(jnp.maximum, ranked))

    most_ranked = route(False)

    @pl.when(most_ranked > PEER_TOPK)
    def _():
        route(True)


def _peer_route(q, keys, tt=LANES, heads_per_step=8):
    n = q.shape[0]
    heads, _, nk, half = keys.shape
    hp = heads_per_step
    out = lambda dtype: jax.ShapeDtypeStruct((heads, nk, n), dtype)
    spec = pl.BlockSpec((hp, nk, tt), lambda i, g: (g, 0, i))
    return pl.pallas_call(
        _peer_route_kernel,
        grid=(n // tt, heads // hp),
        in_specs=[pl.BlockSpec((tt, 2 * half * hp), lambda i, g: (i, g)),
                  pl.BlockSpec((hp, 2, nk, half), lambda i, g: (g, 0, 0, 0))],
        out_specs=[spec] * 4,
        out_shape=[out(BF16), out(F32), out(F32), out(BF16)],
        compiler_params=_params("arbitrary", "arbitrary"),
        name="peer_route",
    )(q, keys)


def _peer_expert_kernel(x_ref, u_ref, vta_ref, vtb_ref, r2_ref, e2_ref, c1_ref, e1_ref, o_ref,
                        pa_ref, pb_ref, *, lane_chunk, sub_experts):
    e = pl.program_id(1)
    last = pl.num_programs(1) - 1
    nk = PEER_KEYS
    tb = x_ref.shape[0]
    half = u_ref.shape[0] // 2
    n_sub = half // sub_experts
    d_rows = vta_ref.shape[0] // n_sub

    def rows_to_tile(ref, h, ii, ts):
        row = jnp.broadcast_to(ref[h, ii:ii + 1, ts], (BF16_SUBLANES, lane_chunk)).astype(BF16)
        return jnp.concatenate([row] * (nk // BF16_SUBLANES), axis=0)

    def evaluate(p_ref, first, c):
        rows = slice(first + c * sub_experts, first + (c + 1) * sub_experts)
        act = _gelu(_dot_nt(u_ref[rows, :], x_ref[...])).astype(BF16)
        for r0 in range(0, sub_experts, nk):
            ii = (rows.start + r0) // nk
            for t0 in range(0, tb, lane_chunk):
                ts = slice(t0, t0 + lane_chunk)
                gate = jnp.zeros((nk, lane_chunk), BF16)
                for h in range(PEER_HEADS):
                    picked = r2_ref[h, :, ts] <= rows_to_tile(c1_ref, h, ii, ts)
                    gate = gate + jnp.where(picked, e2_ref[h, :, ts], 0.0) * rows_to_tile(e1_ref, h, ii, ts)
                p_ref[c * sub_experts + r0:c * sub_experts + r0 + nk, ts] = gate * act[r0:r0 + nk, ts]

    def apply(vt_ref, p_ref, c):
        rows = slice(c * d_rows, (c + 1) * d_rows)
        o_ref[rows, :] += _dot(vt_ref[rows, :], p_ref[...])

    @pl.when(e == 0)
    def _():
        o_ref[...] = jnp.zeros_like(o_ref)
        pb_ref[...] = jnp.zeros_like(pb_ref)

    @pl.when(e < last)
    def _():
        for c in range(n_sub):
            evaluate(pa_ref, 0, c)
            apply(vta_ref, pb_ref, c)
        for c in range(n_sub):
            evaluate(pb_ref, half, c)
            apply(vtb_ref, pa_ref, c)

    @pl.when(e == last)
    def _():
        for c in range(n_sub):
            apply(vta_ref, pb_ref, c)


def _peer_experts(x, u, vt, r2, e2, c1, e1, tb=512, eb=1024, lane_chunk=256, sub_experts=256):
    n, d = x.shape
    ne = u.shape[0] // eb
    heads, nk, _ = r2.shape
    tb = min(tb, n)
    rows_i = eb // nk
    half = eb // 2
    tok = lambda t, e: (0, 0, t)
    cur = lambda e: jnp.minimum(e, ne - 1)
    return pl.pallas_call(
        functools.partial(_peer_expert_kernel, lane_chunk=min(lane_chunk, tb), sub_experts=sub_experts),
        grid=(n // tb, ne + 1),
        in_specs=[pl.BlockSpec((tb, d), lambda t, e: (t, 0)),
                  pl.BlockSpec((eb, d), lambda t, e: (cur(e), 0)),
                  pl.BlockSpec((d, half), lambda t, e: (0, jnp.maximum(2 * e - 1, 0))),
                  pl.BlockSpec((d, half), lambda t, e: (0, 2 * cur(e))),
                  pl.BlockSpec((heads, nk, tb), tok),
                  pl.BlockSpec((heads, nk, tb), tok),
                  pl.BlockSpec((heads, rows_i, tb), lambda t, e: (0, cur(e), t)),
                  pl.BlockSpec((heads, rows_i, tb), lambda t, e: (0, cur(e), t))],
        out_specs=pl.BlockSpec((d, tb), lambda t, e: (0, t)),
        out_shape=jax.ShapeDtypeStruct((d, n), F32),
        scratch_shapes=[pltpu.VMEM((half, tb), BF16), pltpu.VMEM((half, tb), BF16)],
        compiler_params=_params("arbitrary", "arbitrary"),
        name="peer_experts",
    )(x, u, vt, vt, r2, e2, c1, e1)


def _residual_ln_t_kernel(h_ref, yt_ref, g_ref, b_ref, o_ref, ob_ref):
    out = _layer_norm(ALPHA * h_ref[...] + yt_ref[...].T, g_ref[...], b_ref[...])
    o_ref[...] = out
    ob_ref[...] = out.astype(BF16)


def _residual_ln_t(h, yt, g, b, tm=512):
    n, d = h.shape
    tm = min(tm, n)
    row = lambda i: (i, 0)
    fixed = lambda i: (0, 0)
    return pl.pallas_call(
        _residual_ln_t_kernel,
        grid=(n // tm,),
        in_specs=[pl.BlockSpec((tm, d), row), pl.BlockSpec((d, tm), lambda i: (0, i)),
                  pl.BlockSpec((1, d), fixed), pl.BlockSpec((1, d), fixed)],
        out_specs=[pl.BlockSpec((tm, d), row), pl.BlockSpec((tm, d), row)],
        out_shape=[jax.ShapeDtypeStruct((n, d), F32), jax.ShapeDtypeStruct((n, d), BF16)],
        compiler_params=_params("arbitrary"),
        name="residual_ln_t",
    )(h, yt, g.reshape(1, d), b.reshape(1, d))


def _peer_layer(h, hb, w_q, sub_keys, u_tab, v_tab, ln_g, ln_b):
    q = _matmul(hb, w_q.astype(BF16), BF16, name="peer_query")
    r2, c1, e1, e2 = _peer_route(q, sub_keys.astype(BF16))
    yt = _peer_experts(hb, u_tab.astype(BF16), v_tab.astype(BF16).T, r2, e2, c1, e1)
    return _residual_ln_t(h, yt, ln_g, ln_b)


def _sb_mixer(hb, batch, seq, w_in, w_out):
    qkv = _matmul(hb, w_in.astype(BF16), BF16, name="sb_in")
    return _sb_attention(qkv, batch, seq), w_out


def _ml_mixer(hb, batch, seq, w_in, b_gates, norm_g, w_out):
    heads = ML_HEADS
    d_gate = w_out.shape[0]
    n_qkv = w_in.shape[1] - d_gate - 2 * heads
    qkv = _matmul(hb, w_in[:, :n_qkv].astype(BF16), BF16, name="ml_in_qkv")
    w_og = jnp.pad(w_in[:, n_qkv:], ((0, 0), (0, LANES - 2 * heads))).astype(BF16)
    og = _matmul(hb, w_og, F32, tm=512, tn=w_og.shape[1], name="ml_in_gates")
    bias = jnp.pad(b_gates.astype(F32), (0, LANES - 2 * heads)).reshape(1, LANES)
    return _mlstm(qkv, og, bias, norm_g, batch, seq), w_out


def _sg_mixer(hb, batch, seq, w_in, norm_g, norm_b, w_s, b_s, w_out):
    uv = _matmul(hb, w_in.astype(BF16), F32, act="gelu", name="sg_in")
    return _sgu(uv, norm_g, norm_b, w_s, b_s), w_out


def kernel(x, l0_sb_w_in, l0_sb_w_out, l0_ln1_g, l0_ln1_b, l0_peer_w_q, l0_peer_sub_keys, l0_peer_u, l0_peer_v, l0_ln2_g, l0_ln2_b, l1_ml_w_in, l1_ml_b_gates, l1_ml_norm_g, l1_ml_w_out, l1_ln1_g, l1_ln1_b, l1_peer_w_q, l1_peer_sub_keys, l1_peer_u, l1_peer_v, l1_ln2_g, l1_ln2_b, l2_sg_w_in, l2_sg_norm_g, l2_sg_norm_b, l2_sg_w_s, l2_sg_b_s, l2_sg_w_out, l2_ln1_g, l2_ln1_b, l2_peer_w_q, l2_peer_sub_keys, l2_peer_u, l2_peer_v, l2_ln2_g, l2_ln2_b, l3_sb_w_in, l3_sb_w_out, l3_ln1_g, l3_ln1_b, l3_peer_w_q, l3_peer_sub_keys, l3_peer_u, l3_peer_v, l3_ln2_g, l3_ln2_b):
    batch, seq, d = x.shape
    mixers = (
        (_sb_mixer, (l0_sb_w_in, l0_sb_w_out)),
        (_ml_mixer, (l1_ml_w_in, l1_ml_b_gates, l1_ml_norm_g, l1_ml_w_out)),
        (_sg_mixer, (l2_sg_w_in, l2_sg_norm_g, l2_sg_norm_b, l2_sg_w_s, l2_sg_b_s, l2_sg_w_out)),
        (_sb_mixer, (l3_sb_w_in, l3_sb_w_out)),
    )
    norm1 = ((l0_ln1_g, l0_ln1_b), (l1_ln1_g, l1_ln1_b), (l2_ln1_g, l2_ln1_b), (l3_ln1_g, l3_ln1_b))
    peers = (
        (l0_peer_w_q, l0_peer_sub_keys, l0_peer_u, l0_peer_v, l0_ln2_g, l0_ln2_b),
        (l1_peer_w_q, l1_peer_sub_keys, l1_peer_u, l1_peer_v, l1_ln2_g, l1_ln2_b),
        (l2_peer_w_q, l2_peer_sub_keys, l2_peer_u, l2_peer_v, l2_ln2_g, l2_ln2_b),
        (l3_peer_w_q, l3_peer_sub_keys, l3_peer_u, l3_peer_v, l3_ln2_g, l3_ln2_b),
    )
    h = x.reshape(batch * seq, d)
    hb = h.astype(BF16)
    for (mixer, mixer_params), (g1, b1), peer_params in zip(mixers, norm1, peers):
        pre, w_out = mixer(hb, batch, seq, *mixer_params)
        h, hb = _proj_residual_ln(pre, w_out.astype(BF16), h, g1, b1)
        h, hb = _peer_layer(h, hb, *peer_params)
    return h.reshape(batch, seq, d)
```
